```python
import jax, jax.numpy as jnp
from jax import lax
import numpy as np

D_MODEL = 1024
BATCH = 32
SEQ = 2048
DEPTH = 2

N_META = 16
HEAD_DIM = 64
CONV_WIDTH = D_MODEL // 4
CONV_HEADS = CONV_WIDTH // HEAD_DIM
CONV_K = 3
POOL_WIDTH = D_MODEL // 4
POOL_WINDOWS = (2, 4, 8, 16)
N_POOL_GROUPS = len(POOL_WINDOWS)
PG = POOL_WIDTH // N_POOL_GROUPS
ATTN_WIDTH = D_MODEL // 2
ATTN_HEADS = ATTN_WIDTH // HEAD_DIM
MIX_WIDTH = CONV_WIDTH + POOL_WIDTH + ATTN_WIDTH
IN_WIDTH = 3 * CONV_WIDTH + POOL_WIDTH + 3 * ATTN_WIDTH
D_FF = 4 * D_MODEL
Q_BLOCK = 128
EPS = 1e-6
SPLITS = (CONV_WIDTH, 2 * CONV_WIDTH, 3 * CONV_WIDTH,
          3 * CONV_WIDTH + POOL_WIDTH,
          3 * CONV_WIDTH + POOL_WIDTH + ATTN_WIDTH,
          3 * CONV_WIDTH + POOL_WIDTH + 2 * ATTN_WIDTH)

kernel_name = "hybrid_conv_pool_stickbreak_trunk"


def rms_norm(x, g):
    xf = x.astype(jnp.float32)
    y = xf * lax.rsqrt(jnp.mean(xf * xf, axis=-1, keepdims=True) + EPS)
    return (y * g.astype(jnp.float32)).astype(x.dtype)


def causal_dwconv(u, w):
    c = u.shape[-1]
    return lax.conv_general_dilated(
        u, w[:, None, :].astype(u.dtype), window_strides=(1,),
        padding=[(CONV_K - 1, 0)], dimension_numbers=("NWC", "WIO", "NWC"),
        feature_group_count=c)


def multiscale_pool(u, w_grp, scale):
    b, l, _ = u.shape
    ug = u.reshape(b, l, N_POOL_GROUPS, PG).astype(jnp.float32)
    cs = jnp.concatenate([jnp.zeros((b, 1, N_POOL_GROUPS, PG), jnp.float32),
                          lax.cumsum(ug, axis=1)], axis=1)
    t = jnp.arange(l)
    means = []
    for g, w in enumerate(POOL_WINDOWS):
        lo = jnp.maximum(t + 1 - w, 0)
        s = cs[:, 1:, g] - cs[:, lo, g]
        cnt = (t + 1 - lo).astype(jnp.float32)
        means.append(s / cnt[None, :, None])
    pooled = (jnp.stack(means, axis=2) - ug).astype(u.dtype)
    y = jnp.einsum("blgc,gcd->blgd", pooled, w_grp)
    return y.reshape(b, l, POOL_WIDTH) * scale


def stick_breaking_attention(q, k, v):
    l = q.shape[2]
    starts = [0] + list(range(N_META, l, Q_BLOCK))
    ends = starts[1:] + [l]
    scale = HEAD_DIM ** -0.5
    outs = []
    for start, end in zip(starts, ends):
        qb = q[:, :, start:end].astype(jnp.float32)
        kb = k[:, :, :end].astype(jnp.float32)
        z = jnp.einsum("bhqd,bhkd->bhqk", qb, kb) * scale
        mask = jnp.arange(end)[None, :] < jnp.arange(start, end)[:, None]
        log_keep = jnp.where(mask, jax.nn.log_sigmoid(-z), 0.0)
        between = lax.cumsum(log_keep, axis=3, reverse=True) - log_keep
        a = jnp.where(mask, jnp.exp(jax.nn.log_sigmoid(z) + between), 0.0)
        outs.append(jnp.einsum("bhqk,bhkd->bhqd", a, v[:, :, :end].astype(jnp.float32)))
    return jnp.concatenate(outs, axis=2).astype(v.dtype)


def setup_inputs(seed: int = 0) -> dict:
    key = jax.random.key(seed)
    ks = jax.random.split(key, 12)
    f32 = jnp.float32
    nrm = lambda k, s: jax.random.normal(k, s, f32)
    return {
        "x": nrm(ks[0], (BATCH, SEQ, D_MODEL)),
        "meta_tokens": nrm(ks[1], (N_META, D_MODEL)),
        "g_mix": 1.0 + 0.02 * nrm(ks[2], (DEPTH, D_MODEL)),
        "w_in": nrm(ks[3], (DEPTH, D_MODEL, IN_WIDTH)) * D_MODEL ** -0.5,
        "w_conv": nrm(ks[4], (DEPTH, CONV_K, CONV_WIDTH)) * CONV_K ** -0.5,
        "w_pool": nrm(ks[5], (DEPTH, N_POOL_GROUPS, PG, PG)) * PG ** -0.5,
        "pool_scale": 1.0 + 0.02 * nrm(ks[6], (DEPTH, POOL_WIDTH)),
        "w_out": nrm(ks[7], (DEPTH, MIX_WIDTH, D_MODEL)) * MIX_WIDTH ** -0.5,
        "g_mlp": 1.0 + 0.02 * nrm(ks[8], (DEPTH, D_MODEL)),
        "w_up": nrm(ks[9], (DEPTH, D_MODEL, D_FF)) * D_MODEL ** -0.5,
        "w_down": nrm(ks[10], (DEPTH, D_FF, D_MODEL)) * D_FF ** -0.5,
        "g_final": 1.0 + 0.02 * nrm(ks[11], (D_MODEL,)),
    }


def reference(x, meta_tokens, g_mix, w_in, w_conv, w_pool, pool_scale, w_out,
              g_mlp, w_up, w_down, g_final):
    b = x.shape[0]
    meta = jnp.broadcast_to(meta_tokens[None].astype(x.dtype), (b, N_META, D_MODEL))
    h = jnp.concatenate([meta, x], axis=1)
    l = h.shape[1]
    to_heads = lambda t: t.reshape(b, l, ATTN_HEADS, HEAD_DIM).transpose(0, 2, 1, 3)
    for i in range(DEPTH):
        u = rms_norm(h, g_mix[i]) @ w_in[i]
        c_b, c_c, c_x, p_in, q, k, v = jnp.split(u, SPLITS, axis=-1)
        y_conv = c_b * causal_dwconv(c_c * c_x, w_conv[i])
        y_pool = multiscale_pool(p_in, w_pool[i], pool_scale[i])
        y_attn = stick_breaking_attention(to_heads(q), to_heads(k), to_heads(v))
        y_attn = y_attn.transpose(0, 2, 1, 3).reshape(b, l, ATTN_WIDTH)
        h = h + jnp.concatenate([y_conv, y_pool, y_attn], axis=-1) @ w_out[i]
        m = rms_norm(h, g_mlp[i]) @ w_up[i]
        h = h + jnp.square(jax.nn.relu(m)) @ w_down[i]
    return rms_norm(h, g_final)[:, N_META:]
```

```python
import functools

import jax
import jax.numpy as jnp
from jax import lax
from jax.experimental import pallas as pl
from jax.experimental.pallas import tpu as pltpu

N_META = 16
HEAD_DIM = 64
CONV_K = 3
POOL_WINDOWS = (2, 4, 8, 16)
EPS = 1e-6

LANES = 128
HALO = 16
ATT_BLK = 128
VMEM_LIMIT = 56 * 1024 * 1024

_BF16 = jnp.bfloat16
_F32 = jnp.float32


def _token_tile(n_tokens, cap=512):
    best = None
    for t in range(16, cap + 1, 16):
        if n_tokens % t == 0:
            best = t
    assert best is not None, n_tokens
    return best


def _rms_scale(x):
    return lax.rsqrt(jnp.mean(x * x, axis=-1, keepdims=True) + EPS)


def _inproj_kernel(h_ref, g_ref, wa_ref, wqkv_ref, wconv_ref, wpool_ref, pscale_ref,
                   ycp_ref, q_ref, k_ref, v_ref, buf_ref, lvl_ref, *, tm, seq_len, cw, pw, aw):
    i = pl.program_id(0)
    top = 2 * HALO

    @pl.when(i == 0)
    def _():
        buf_ref[0:top, :] = jnp.zeros((top, cw + pw), _F32)
        lvl_ref[0:top, :] = jnp.zeros((top, pw), _F32)

    x = h_ref[...]
    xn = (x * _rms_scale(x) * g_ref[...]).astype(_BF16)

    ua = jnp.dot(xn, wa_ref[...], preferred_element_type=_F32)
    c_b = ua[:, 0:cw]
    prod = ua[:, cw:2 * cw] * ua[:, 2 * cw:3 * cw]
    p_in = ua[:, 3 * cw:3 * cw + pw]
    buf_ref[top:top + tm, 0:cw] = prod
    buf_ref[top:top + tm, cw:cw + pw] = p_in

    base = lax.rem(i * tm, seq_len)
    tpos = base + lax.broadcasted_iota(jnp.int32, (tm, 1), 0)
    tpos = jnp.where(tpos >= seq_len, tpos - seq_len, tpos)

    wc = wconv_ref[...]
    conv = wc[CONV_K - 1:CONV_K, :] * prod
    for s in range(1, CONV_K):
        shifted = buf_ref[top - s:top - s + tm, 0:cw]
        conv = conv + jnp.where(tpos >= s, shifted, 0.0) * wc[CONV_K - 1 - s:CONV_K - s, :]
    y_conv = c_b * conv

    ext = tm + HALO
    epos = tpos[0:1, :] - HALO + lax.broadcasted_iota(jnp.int32, (ext, 1), 0)
    epos = jnp.where(epos < 0, epos + seq_len, epos)
    epos = jnp.where(epos >= seq_len, epos - seq_len, epos)
    lane = lax.broadcasted_iota(jnp.int32, (1, pw), 1)
    pg = pw // len(POOL_WINDOWS)
    win = jnp.zeros((1, pw), jnp.int32)
    for gi, w in enumerate(POOL_WINDOWS):
        win = jnp.where((lane >= gi * pg) & (lane < (gi + 1) * pg), w, win)

    cur = buf_ref[HALO:HALO + ext, cw:cw + pw]
    sums = cur
    shift = 1
    while shift < max(POOL_WINDOWS):
        if shift == 1:
            prev = buf_ref[HALO - 1:HALO - 1 + ext, cw:cw + pw]
        else:
            lvl_ref[HALO:HALO + ext, :] = cur
            prev = lvl_ref[HALO - shift:HALO - shift + ext, :]
        cur = cur + jnp.where(epos >= shift, prev, 0.0)
        shift *= 2
        sums = jnp.where(win >= shift, cur, sums)
    sums = sums[HALO:, :]
    cnt = jnp.minimum(win, tpos + 1).astype(_F32)
    pooled = (sums / cnt - p_in).astype(_BF16)
    y_pool = jnp.dot(pooled, wpool_ref[...], preferred_element_type=_F32) * pscale_ref[...]

    ycp_ref[:, 0:cw] = y_conv.astype(_BF16)
    ycp_ref[:, cw:cw + pw] = y_pool.astype(_BF16)

    buf_ref[HALO:top, :] = buf_ref[tm + HALO:tm + top, :]

    uq = jnp.dot(xn, wqkv_ref[...], preferred_element_type=_F32)
    q_ref[...] = (uq[:, 0:aw] * (HEAD_DIM ** -0.5)).astype(_BF16)
    k_ref[...] = uq[:, aw:2 * aw].astype(_BF16)
    v_ref[...] = uq[:, 2 * aw:3 * aw].astype(_BF16)


def _const_spec(shape):
    return pl.BlockSpec(shape, lambda *_: (0,) * len(shape), pipeline_mode=pl.Buffered(1))


def _inproj_call(h, g, wa, wqkv, wconv, wpool, pscale, *, seq_len):
    n_tok, d = h.shape
    cw = wconv.shape[1]
    pw = wpool.shape[0]
    aw = wqkv.shape[1] // 3
    tm = _token_tile(n_tok)
    assert HALO <= tm <= seq_len
    kern = functools.partial(_inproj_kernel, tm=tm, seq_len=seq_len, cw=cw, pw=pw, aw=aw)
    row = lambda width: pl.BlockSpec((tm, width), lambda i: (i, 0))
    return pl.pallas_call(
        kern,
        grid=(n_tok // tm,),
        in_specs=[row(d), _const_spec(g.shape), _const_spec(wa.shape), _const_spec(wqkv.shape),
                  _const_spec(wconv.shape), _const_spec(wpool.shape), _const_spec(pscale.shape)],
        out_specs=[row(cw + pw), row(aw), row(aw), row(aw)],
        out_shape=[jax.ShapeDtypeStruct((n_tok, cw + pw), _BF16)]
        + [jax.ShapeDtypeStruct((n_tok, aw), _BF16)] * 3,
        scratch_shapes=[pltpu.VMEM((tm + 2 * HALO, cw + pw), _F32),
                        pltpu.VMEM((tm + 2 * HALO, pw), _F32)],
        compiler_params=pltpu.CompilerParams(dimension_semantics=("arbitrary",),
                                             vmem_limit_bytes=VMEM_LIMIT),
        name="inproj_mix",
    )(h, g, wa, wqkv, wconv, wpool, pscale)


def _softplus(z):
    return jnp.maximum(z, 0.0) + jnp.log(1.0 + jnp.exp(-jnp.abs(z)))


def _attn_kernel(q_ref, k_ref, v_ref, tri_ref, o_ref, *, seq_len):
    blk = ATT_BLK
    n_blk = (seq_len - N_META) // blk
    lane = lax.broadcasted_iota(jnp.int32, (1, 2 * HEAD_DIM), 1)
    head_lanes = [lane < HEAD_DIM, lane >= HEAD_DIM]
    row_i = lax.broadcasted_iota(jnp.int32, (blk, blk), 0)
    col_i = lax.broadcasted_iota(jnp.int32, (blk, blk), 1)
    causal = col_i < row_i
    meta_only = col_i < N_META
    tri = tri_ref[...]

    def tile(carry, qms, k0, mask):
        acc, rsum = carry
        kb = k_ref[pl.ds(k0, blk), :]
        vb = v_ref[pl.ds(k0, blk), :]
        new_rsum = []
        for hd in range(2):
            z = lax.dot_general(qms[hd], kb, (((1,), (1,)), ((), ())), preferred_element_type=_F32)
            sp = _softplus(z)
            if mask is not None:
                sp = jnp.where(mask, sp, 0.0)
            hi = sp.astype(_BF16)
            lo = (sp - hi.astype(_F32)).astype(_BF16)
            incl = jnp.dot(jnp.concatenate([hi, lo], axis=1), tri, preferred_element_type=_F32)
            a = jnp.exp(z - incl - rsum[hd])
            if mask is not None:
                a = jnp.where(mask, a, 0.0)
            vh = jnp.where(head_lanes[hd], vb, jnp.zeros_like(vb))
            acc = acc + jnp.dot(a.astype(_BF16), vh, preferred_element_type=_F32)
            new_rsum.append(rsum[hd] + incl[:, 0:1])
        return acc, tuple(new_rsum)

    def query_block(q0, n_full):
        q2 = q_ref[pl.ds(q0, blk), :]
        qms = [jnp.where(m, q2, jnp.zeros_like(q2)) for m in head_lanes]
        zero_r = jnp.zeros((blk, 1), _F32)
        carry = (jnp.zeros((blk, 2 * HEAD_DIM), _F32), (zero_r, zero_r))
        carry = tile(carry, qms, q0, causal)

        def body(t, c):
            k0 = pl.multiple_of(q0 - (t + 1) * blk, N_META)
            return tile(c, qms, k0, None)

        carry = lax.fori_loop(0, n_full, body, carry)
        return carry, qms

    (acc, _), _ = query_block(0, 0)
    o_ref[0:blk, :] = acc.astype(o_ref.dtype)

    def q_loop(i, _):
        q0 = pl.multiple_of(N_META + i * blk, N_META)
        carry, qms = query_block(q0, i)
        acc, _ = tile(carry, qms, 0, meta_only)
        o_ref[pl.ds(q0, blk), :] = acc.astype(o_ref.dtype)
        return 0

    lax.fori_loop(0, n_blk, q_loop, 0)


def _attn_call(q, k, v, tri):
    b, seq_len, aw = q.shape
    assert (seq_len - N_META) % ATT_BLK == 0 and seq_len >= ATT_BLK
    pair = 2 * HEAD_DIM
    spec = pl.BlockSpec((None, seq_len, pair), lambda bi, pi: (bi, 0, pi))
    return pl.pallas_call(
        functools.partial(_attn_kernel, seq_len=seq_len),
        grid=(b, aw // pair),
        in_specs=[spec, spec, spec, _const_spec(tri.shape)],
        out_specs=spec,
        out_shape=jax.ShapeDtypeStruct((b, seq_len, aw), _BF16),
        compiler_params=pltpu.CompilerParams(dimension_semantics=("parallel", "parallel"),
                                             vmem_limit_bytes=VMEM_LIMIT),
        name="stickbreak_attn",
    )(q, k, v, tri)


def _outmlp_kernel(h_ref, ycp_ref, ya_ref, wo_ref, g_ref, wup_ref, wdn_ref, gf_ref, o_ref,
                   *, ff_chunk, final_norm):
    n_cp = ycp_ref.shape[1]
    h1 = h_ref[...]
    h1 = h1 + jnp.dot(ycp_ref[...], wo_ref[0:n_cp, :], preferred_element_type=_F32)
    h1 = h1 + jnp.dot(ya_ref[...], wo_ref[n_cp:, :], preferred_element_type=_F32)
    xn = (h1 * _rms_scale(h1) * g_ref[...]).astype(_BF16)
    o_ref[...] = h1
    for c in range(wup_ref.shape[1] // ff_chunk):
        cols = slice(c * ff_chunk, (c + 1) * ff_chunk)
        m = jnp.dot(xn, wup_ref[:, cols], preferred_element_type=_F32)
        act = jnp.square(jnp.maximum(m, 0.0)).astype(_BF16)
        o_ref[...] += jnp.dot(act, wdn_ref[cols, :], preferred_element_type=_F32)
    if final_norm:
        out = o_ref[...]
        o_ref[...] = out * _rms_scale(out) * gf_ref[...]


def _outmlp_call(h, ycp, ya, wo, g, wup, wdn, gf, *, final_norm):
    n_tok, d = h.shape
    tm = _token_tile(n_tok)
    row = lambda width: pl.BlockSpec((tm, width), lambda i: (i, 0))
    kern = functools.partial(_outmlp_kernel, ff_chunk=512, final_norm=final_norm)
    return pl.pallas_call(
        kern,
        grid=(n_tok // tm,),
        in_specs=[row(d), row(ycp.shape[1]), row(ya.shape[1]), _const_spec(wo.shape),
                  _const_spec(g.shape), _const_spec(wup.shape), _const_spec(wdn.shape),
                  _const_spec(gf.shape)],
        out_specs=row(d),
        out_shape=jax.ShapeDtypeStruct((n_tok, d), _F32),
        compiler_params=pltpu.CompilerParams(dimension_semantics=("parallel",),
                                             vmem_limit_bytes=VMEM_LIMIT),
        name="outproj_mlp",
    )(h, ycp, ya, wo, g, wup, wdn, gf)


def _block_diag(w_grp):
    g, pg, _ = w_grp.shape
    out = jnp.zeros((g * pg, g * pg), w_grp.dtype)
    for gi in range(g):
        out = out.at[gi * pg:(gi + 1) * pg, gi * pg:(gi + 1) * pg].set(w_grp[gi])
    return out


def kernel(x, meta_tokens, g_mix, w_in, w_conv, w_pool, pool_scale, w_out, g_mlp, w_up, w_down, g_final):
    b, seq, d = x.shape
    depth = w_in.shape[0]
    cw = w_conv.shape[2]
    pw = pool_scale.shape[1]
    n_a = 3 * cw + pw
    meta = jnp.broadcast_to(meta_tokens[None].astype(x.dtype), (b, N_META, d))
    seq_len = N_META + seq
    h = jnp.concatenate([meta, x], axis=1).reshape(b * seq_len, d)

    j = lax.broadcasted_iota(jnp.int32, (2 * ATT_BLK, ATT_BLK), 0) % ATT_BLK
    s = lax.broadcasted_iota(jnp.int32, (2 * ATT_BLK, ATT_BLK), 1)
    tri = (j >= s).astype(_BF16)

    for i in range(depth):
        wa = w_in[i, :, :n_a].astype(_BF16)
        wqkv = w_in[i, :, n_a:].astype(_BF16)
        ycp, q, k, v = _inproj_call(
            h, g_mix[i][None], wa, wqkv, w_conv[i], _block_diag(w_pool[i]).astype(_BF16),
            pool_scale[i][None], seq_len=seq_len)
        aw = q.shape[1]
        to_seq = lambda t: t.reshape(b, seq_len, aw)
        ya = _attn_call(to_seq(q), to_seq(k), to_seq(v), tri).reshape(b * seq_len, aw)
        h = _outmlp_call(h, ycp, ya, w_out[i].astype(_BF16), g_mlp[i][None],
                         w_up[i].astype(_BF16), w_down[i].astype(_BF16), g_final[None],
                         final_norm=(i == depth - 1))
    return h.reshape(b, seq_len, d)[:, N_META:]
```

```python
import functools

import jax
import jax.numpy as jnp
from jax import lax
from jax.experimental import pallas as pl
from jax.experimental.pallas import tpu as pltpu

N_META = 16
HEAD_DIM = 64
CONV_K = 3
POOL_WINDOWS = (2, 4, 8, 16)
EPS = 1e-6

LANES = 128
HALO = 16
ATT_BLK = 128
ATT_NEAR = 2
ATT_PIPE = 3
ATT_DEAD_LOG2 = 152.0
ATT_MASKED = -1e30
LOG2_E = 1.4426950408889634
VMEM_LIMIT = 56 * 1024 * 1024

_BF16 = jnp.bfloat16
_F32 = jnp.float32


def _token_tile(n_tokens, cap=512):
    best = None
    for t in range(16, cap + 1, 16):
        if n_tokens % t == 0:
            best = t
    assert best is not None, n_tokens
    return best


def _rms_scale(x):
    return lax.rsqrt(jnp.mean(x * x, axis=-1, keepdims=True) + EPS)


def _inproj_kernel(h_ref, g_ref, wa_ref, wqkv_ref, wconv_ref, wpool_ref, pscale_ref,
                   ycp_ref, q_ref, k_ref, v_ref, buf_ref, lvl_ref, *, tm, seq_len, cw, pw, aw):
    i = pl.program_id(0)
    top = 2 * HALO

    @pl.when(i == 0)
    def _():
        buf_ref[0:top, :] = jnp.zeros((top, cw + pw), _F32)
        lvl_ref[0:top, :] = jnp.zeros((top, pw), _F32)

    x = h_ref[...]
    xn = (x * _rms_scale(x) * g_ref[...]).astype(_BF16)

    ua = jnp.dot(xn, wa_ref[...], preferred_element_type=_F32)
    c_b = ua[:, 0:cw]
    prod = ua[:, cw:2 * cw] * ua[:, 2 * cw:3 * cw]
    p_in = ua[:, 3 * cw:3 * cw + pw]
    buf_ref[top:top + tm, 0:cw] = prod
    buf_ref[top:top + tm, cw:cw + pw] = p_in

    base = lax.rem(i * tm, seq_len)
    tpos = base + lax.broadcasted_iota(jnp.int32, (tm, 1), 0)
    tpos = jnp.where(tpos >= seq_len, tpos - seq_len, tpos)

    wc = wconv_ref[...]
    conv = wc[CONV_K - 1:CONV_K, :] * prod
    for s in range(1, CONV_K):
        shifted = buf_ref[top - s:top - s + tm, 0:cw]
        conv = conv + jnp.where(tpos >= s, shifted, 0.0) * wc[CONV_K - 1 - s:CONV_K - s, :]
    y_conv = c_b * conv

    ext = tm + HALO
    epos = tpos[0:1, :] - HALO + lax.broadcasted_iota(jnp.int32, (ext, 1), 0)
    epos = jnp.where(epos < 0, epos + seq_len, epos)
    epos = jnp.where(epos >= seq_len, epos - seq_len, epos)
    lane = lax.broadcasted_iota(jnp.int32, (1, pw), 1)
    pg = pw // len(POOL_WINDOWS)
    win = jnp.zeros((1, pw), jnp.int32)
    for gi, w in enumerate(POOL_WINDOWS):
        win = jnp.where((lane >= gi * pg) & (lane < (gi + 1) * pg), w, win)

    cur = buf_ref[HALO:HALO + ext, cw:cw + pw]
    sums = cur
    shift = 1
    while shift < max(POOL_WINDOWS):
        if shift == 1:
            prev = buf_ref[HALO - 1:HALO - 1 + ext, cw:cw + pw]
        else:
            lvl_ref[HALO:HALO + ext, :] = cur
            prev = lvl_ref[HALO - shift:HALO - shift + ext, :]
        cur = cur + jnp.where(epos >= shift, prev, 0.0)
        shift *= 2
        sums = jnp.where(win >= shift, cur, sums)
    sums = sums[HALO:, :]
    cnt = jnp.minimum(win, tpos + 1).astype(_F32)
    pooled = (sums / cnt - p_in).astype(_BF16)
    y_pool = jnp.dot(pooled, wpool_ref[...], preferred_element_type=_F32) * pscale_ref[...]

    ycp_ref[:, 0:cw] = y_conv.astype(_BF16)
    ycp_ref[:, cw:cw + pw] = y_pool.astype(_BF16)

    buf_ref[HALO:top, :] = buf_ref[tm + HALO:tm + top, :]

    uq = jnp.dot(xn, wqkv_ref[...], preferred_element_type=_F32)
    q_ref[...] = (uq[:, 0:aw] * (HEAD_DIM ** -0.5)).astype(_BF16)
    k_ref[...] = uq[:, aw:2 * aw].astype(_BF16)
    v_ref[...] = uq[:, 2 * aw:3 * aw].astype(_BF16)


def _const_spec(shape):
    return pl.BlockSpec(shape, lambda *_: (0,) * len(shape), pipeline_mode=pl.Buffered(1))


def _inproj_call(h, g, wa, wqkv, wconv, wpool, pscale, *, seq_len):
    n_tok, d = h.shape
    cw = wconv.shape[1]
    pw = wpool.shape[0]
    aw = wqkv.shape[1] // 3
    tm = _token_tile(n_tok)
    assert HALO <= tm <= seq_len
    kern = functools.partial(_inproj_kernel, tm=tm, seq_len=seq_len, cw=cw, pw=pw, aw=aw)
    row = lambda width: pl.BlockSpec((tm, width), lambda i: (i, 0))
    return pl.pallas_call(
        kern,
        grid=(n_tok // tm,),
        in_specs=[row(d), _const_spec(g.shape), _const_spec(wa.shape), _const_spec(wqkv.shape),
                  _const_spec(wconv.shape), _const_spec(wpool.shape), _const_spec(pscale.shape)],
        out_specs=[row(cw + pw), row(aw), row(aw), row(aw)],
        out_shape=[jax.ShapeDtypeStruct((n_tok, cw + pw), _BF16)]
        + [jax.ShapeDtypeStruct((n_tok, aw), _BF16)] * 3,
        scratch_shapes=[pltpu.VMEM((tm + 2 * HALO, cw + pw), _F32),
                        pltpu.VMEM((tm + 2 * HALO, pw), _F32)],
        compiler_params=pltpu.CompilerParams(dimension_semantics=("arbitrary",),
                                             vmem_limit_bytes=VMEM_LIMIT),
        name="inproj_mix",
    )(h, g, wa, wqkv, wconv, wpool, pscale)


def _softplus2(t):
    return jnp.maximum(t, 0.0) + jnp.log2(1.0 + jnp.exp2(-jnp.abs(t)))


def _hi_lo(x):
    hi = x.astype(_BF16)
    lo = (x - hi.astype(_F32)).astype(_BF16)
    return jnp.concatenate([hi, lo], axis=1)


def _attn_kernel(q_ref, k_ref, v_ref, tri_ref, o_ref, qp_ref, kp_ref, vp_ref, op_ref, rs_ref,
                 *pipe_refs, seq_len):
    blk = ATT_BLK
    near = ATT_NEAR * blk
    front = near + blk - N_META
    n_pad_blk = (front + seq_len) // blk
    lane = lax.broadcasted_iota(jnp.int32, (1, 2 * HEAD_DIM), 1)
    head_lanes = [lane < HEAD_DIM, lane >= HEAD_DIM]
    causal = (lax.broadcasted_iota(jnp.int32, (blk, blk), 1)
              < lax.broadcasted_iota(jnp.int32, (blk, blk), 0))
    nt = (((1,), (1,)), ((), ()))

    for src, dst in ((q_ref, qp_ref), (k_ref, kp_ref), (v_ref, vp_ref)):
        dst[0:front, :] = jnp.zeros((front, 2 * HEAD_DIM), _BF16)
        dst[front:front + seq_len, :] = src[...]

    def by_head(x):
        return jnp.concatenate([jnp.where(m, x, jnp.zeros_like(x)) for m in head_lanes], axis=0)

    def logits(q_heads, k0, n_kb, diag):
        kb = kp_ref[pl.ds(k0, n_kb * blk), :]
        t = lax.dot_general(q_heads, kb, nt, preferred_element_type=_F32) * LOG2_E
        if diag:
            last = jnp.where(jnp.concatenate([causal, causal], axis=0),
                             t[:, (n_kb - 1) * blk:], ATT_MASKED)
            t = last if n_kb == 1 else jnp.concatenate([t[:, :(n_kb - 1) * blk], last], axis=1)
        return t

    def cumsums(t_blk, n_kb):
        parts = [_hi_lo(_softplus2(t_blk(hd, b))) for hd in range(2) for b in range(n_kb)]
        return jnp.dot(jnp.concatenate(parts, axis=0), tri_ref[...], preferred_element_type=_F32)

    def weighted_values(t_blk, r_blk, k0, n_kb, offs):
        ws, new_offs = [], []
        for hd in range(2):
            off = None if offs is None else offs[hd]
            w_hd = [None] * n_kb
            for b in range(n_kb - 1, -1, -1):
                rb = r_blk(hd * n_kb + b)
                arg = t_blk(hd, b) - rb[:, 0:blk]
                if off is not None:
                    arg = arg - off
                w_hd[b] = jnp.exp2(arg)
                off = rb[:, blk:] if off is None else off + rb[:, blk:]
            ws += w_hd
            new_offs.append(off)
        a = jnp.concatenate(ws, axis=1).astype(_BF16)
        vb = vp_ref[pl.ds(k0, n_kb * blk), :]
        return jnp.dot(a, by_head(vb), preferred_element_type=_F32), new_offs

    def span(q_heads, k0, n_kb, diag, offs):
        t = logits(q_heads, k0, n_kb, diag)
        t_blk = lambda hd, b: t[hd * blk:(hd + 1) * blk, b * blk:(b + 1) * blk]
        r = cumsums(t_blk, n_kb)
        return weighted_values(t_blk, lambda n: r[n * blk:(n + 1) * blk, :], k0, n_kb, offs)

    n_kb = ATT_NEAR + 1
    first = ATT_NEAR
    last = n_pad_blk - 1
    t_refs, r_refs = pipe_refs[:ATT_PIPE], pipe_refs[ATT_PIPE:]

    def step_logits(i, slot):
        q0 = pl.multiple_of(i * blk, blk)
        k0 = pl.multiple_of(i * blk - near, blk)
        t_refs[slot][...] = logits(by_head(qp_ref[pl.ds(q0, blk), :]), k0, n_kb, True)

    def t_blk_of(slot):
        return lambda hd, b: t_refs[slot][hd * blk:(hd + 1) * blk, b * blk:(b + 1) * blk]

    def step_cumsums(slot):
        r_refs[slot][...] = cumsums(t_blk_of(slot), n_kb)

    def step_values(i, slot, gmin):
        q0 = pl.multiple_of(i * blk, blk)
        k0 = pl.multiple_of(i * blk - near, blk)
        acc, offs = weighted_values(t_blk_of(slot),
                                    lambda n: r_refs[slot][n * blk:(n + 1) * blk, :],
                                    k0, n_kb, None)
        rmin = jnp.minimum(offs[0], offs[1])
        op_ref[pl.ds(q0, blk), :] = acc.astype(_BF16)
        rs_ref[pl.ds(q0, blk), :] = rmin
        return jnp.minimum(gmin, rmin)

    def pipe_steps(g, gmin):
        for u in range(ATT_PIPE):
            i = first + g * ATT_PIPE + u
            step_logits(i + 2, (u + 2) % ATT_PIPE)
            step_cumsums((u + 1) % ATT_PIPE)
            gmin = step_values(i, u, gmin)
        return gmin

    n_steady = last - first - 1
    assert n_steady > 0 and n_steady % ATT_PIPE == 0
    step_logits(first, 0)
    step_logits(first + 1, 1)
    step_cumsums(0)
    gmin = lax.fori_loop(0, n_steady // ATT_PIPE, pipe_steps,
                         jnp.full((blk, blk), jnp.inf, _F32))
    step_cumsums((n_steady + 1) % ATT_PIPE)
    gmin = step_values(last - 1, n_steady % ATT_PIPE, gmin)
    gmin = step_values(last, (n_steady + 1) % ATT_PIPE, gmin)

    def full_pass(i, _):
        q0 = pl.multiple_of(i * blk, blk)

        @pl.when(jnp.min(rs_ref[pl.ds(q0, blk), :]) <= ATT_DEAD_LOG2)
        def _():
            q_heads = by_head(qp_ref[pl.ds(q0, blk), :])
            acc, offs = span(q_heads, q0, 1, True, None)

            def cond(c):
                return (c[0] >= first) & (c[1] > 0)

            def body(c):
                j, _, acc, off_a, off_b = c
                pv, (off_a, off_b) = span(q_heads, pl.multiple_of(j * blk, blk), 1, False,
                                          [off_a, off_b])
                alive = jnp.min(jnp.minimum(off_a, off_b)) <= ATT_DEAD_LOG2
                return j - 1, alive.astype(jnp.int32), acc + pv, off_a, off_b

            c = lax.while_loop(cond, body, (i - 1, jnp.int32(1), acc, offs[0], offs[1]))
            op_ref[pl.ds(q0, blk), :] = c[2].astype(_BF16)

        return 0

    @pl.when(jnp.min(gmin) <= ATT_DEAD_LOG2)
    def _():
        lax.fori_loop(first, n_pad_blk, full_pass, 0)

    o_ref[...] = op_ref[front:front + seq_len, :]


def _attn_call(q, k, v, tri):
    b, seq_len, aw = q.shape
    assert (seq_len - N_META) % ATT_BLK == 0
    pair = 2 * HEAD_DIM
    n_kb = ATT_NEAR + 1
    pad_len = n_kb * ATT_BLK - N_META + seq_len
    spec = pl.BlockSpec((None, seq_len, pair), lambda bi, pi: (bi, 0, pi))
    return pl.pallas_call(
        functools.partial(_attn_kernel, seq_len=seq_len),
        grid=(b, aw // pair),
        in_specs=[spec, spec, spec, _const_spec(tri.shape)],
        out_specs=spec,
        out_shape=jax.ShapeDtypeStruct((b, seq_len, aw), _BF16),
        scratch_shapes=[pltpu.VMEM((pad_len, pair), _BF16)] * 4
        + [pltpu.VMEM((pad_len, ATT_BLK), _F32)]
        + [pltpu.VMEM((2 * ATT_BLK, n_kb * ATT_BLK), _F32)] * ATT_PIPE
        + [pltpu.VMEM((2 * n_kb * ATT_BLK, 2 * ATT_BLK), _F32)] * ATT_PIPE,
        compiler_params=pltpu.CompilerParams(dimension_semantics=("parallel", "parallel"),
                                             vmem_limit_bytes=VMEM_LIMIT),
        name="stickbreak_attn",
    )(q, k, v, tri)


def _outmlp_kernel(h_ref, ycp_ref, ya_ref, wo_ref, g_ref, wup_ref, wdn_ref, gf_ref, o_ref,
                   *, ff_chunk, final_norm):
    n_cp = ycp_ref.shape[1]
    h1 = h_ref[...]
    h1 = h1 + jnp.dot(ycp_ref[...], wo_ref[0:n_cp, :], preferred_element_type=_F32)
    h1 = h1 + jnp.dot(ya_ref[...], wo_ref[n_cp:, :], preferred_element_type=_F32)
    xn = (h1 * _rms_scale(h1) * g_ref[...]).astype(_BF16)
    o_ref[...] = h1
    for c in range(wup_ref.shape[1] // ff_chunk):
        cols = slice(c * ff_chunk, (c + 1) * ff_chunk)
        m = jnp.dot(xn, wup_ref[:, cols], preferred_element_type=_F32)
        act = jnp.square(jnp.maximum(m, 0.0)).astype(_BF16)
        o_ref[...] += jnp.dot(act, wdn_ref[cols, :], preferred_element_type=_F32)
    if final_norm:
        out = o_ref[...]
        o_ref[...] = out * _rms_scale(out) * gf_ref[...]


def _outmlp_call(h, ycp, ya, wo, g, wup, wdn, gf, *, final_norm):
    n_tok, d = h.shape
    tm = _token_tile(n_tok)
    row = lambda width: pl.BlockSpec((tm, width), lambda i: (i, 0))
    kern = functools.partial(_outmlp_kernel, ff_chunk=512, final_norm=final_norm)
    return pl.pallas_call(
        kern,
        grid=(n_tok // tm,),
        in_specs=[row(d), row(ycp.shape[1]), row(ya.shape[1]), _const_spec(wo.shape),
                  _const_spec(g.shape), _const_spec(wup.shape), _const_spec(wdn.shape),
                  _const_spec(gf.shape)],
        out_specs=row(d),
        out_shape=jax.ShapeDtypeStruct((n_tok, d), _F32),
        compiler_params=pltpu.CompilerParams(dimension_semantics=("parallel",),
                                             vmem_limit_bytes=VMEM_LIMIT),
        name="outproj_mlp",
    )(h, ycp, ya, wo, g, wup, wdn, gf)


def _block_diag(w_grp):
    g, pg, _ = w_grp.shape
    out = jnp.zeros((g * pg, g * pg), w_grp.dtype)
    for gi in range(g):
        out = out.at[gi * pg:(gi + 1) * pg, gi * pg:(gi + 1) * pg].set(w_grp[gi])
    return out


def kernel(x, meta_tokens, g_mix, w_in, w_conv, w_pool, pool_scale, w_out, g_mlp, w_up, w_down, g_final):
    b, seq, d = x.shape
    depth = w_in.shape[0]
    cw = w_conv.shape[2]
    pw = pool_scale.shape[1]
    n_a = 3 * cw + pw
    meta = jnp.broadcast_to(meta_tokens[None].astype(x.dtype), (b, N_META, d))
    seq_len = N_META + seq
    h = jnp.concatenate([meta, x], axis=1).reshape(b * seq_len, d)

    j = lax.broadcasted_iota(jnp.int32, (2 * ATT_BLK, 2 * ATT_BLK), 0) % ATT_BLK
    s = lax.broadcasted_iota(jnp.int32, (2 * ATT_BLK, 2 * ATT_BLK), 1)
    tri = ((j >= s) | (s >= ATT_BLK)).astype(_BF16)

    for i in range(depth):
        wa = w_in[i, :, :n_a].astype(_BF16)
        wqkv = w_in[i, :, n_a:].astype(_BF16)
        ycp, q, k, v = _inproj_call(
            h, g_mix[i][None], wa, wqkv, w_conv[i], _block_diag(w_pool[i]).astype(_BF16),
            pool_scale[i][None], seq_len=seq_len)
        aw = q.shape[1]
        to_seq = lambda t: t.reshape(b, seq_len, aw)
        ya = _attn_call(to_seq(q), to_seq(k), to_seq(v), tri).reshape(b * seq_len, aw)
        h = _outmlp_call(h, ycp, ya, w_out[i].astype(_BF16), g_mlp[i][None],
                         w_up[i].astype(_BF16), w_down[i].astype(_BF16), g_final[None],
                         final_norm=(i == depth - 1))
    return h.reshape(b, seq_len, d)[:, N_META:]
```

```python
import functools

import jax
import jax.numpy as jnp
from jax import lax
from jax.experimental import pallas as pl
from jax.experimental.pallas import tpu as pltpu

N_META = 16
HEAD_DIM = 64
CONV_K = 3
POOL_WINDOWS = (2, 4, 8, 16)
EPS = 1e-6

LANES = 128
HALO = 16
ATT_BLK = 128
ATT_NEAR = 2
INPROJ_SPLIT = 2
ATT_PIPE = 3
ATT_DEAD_LOG2 = 152.0
ATT_MASKED = -1e30
LOG2_E = 1.4426950408889634
VMEM_LIMIT = 56 * 1024 * 1024

_BF16 = jnp.bfloat16
_F32 = jnp.float32


def _token_tile(n_tokens, cap=512):
    best = None
    for t in range(16, cap + 1, 16):
        if n_tokens % t == 0:
            best = t
    assert best is not None, n_tokens
    return best


def _rms_scale(x):
    return lax.rsqrt(jnp.mean(x * x, axis=-1, keepdims=True) + EPS)


def _inproj_kernel(h_ref, g_ref, wa_ref, wqkv_ref, wconv_ref, wpool_ref, pscale_ref,
                   ycp_ref, q_ref, k_ref, v_ref, buf_ref, lvl_ref, cb_ref,
                   *, tm, seq_len, cw, pw, aw):
    i = pl.program_id(0)
    top = 2 * HALO

    @pl.when(i == 0)
    def _():
        buf_ref[0:top, :] = jnp.zeros((top, cw + pw), _F32)
        lvl_ref[0:top, :] = jnp.zeros((top, pw), _F32)

    base = lax.rem(i * tm, seq_len)
    lane = lax.broadcasted_iota(jnp.int32, (1, pw), 1)
    pg = pw // len(POOL_WINDOWS)
    win = jnp.zeros((1, pw), jnp.int32)
    for gi, w in enumerate(POOL_WINDOWS):
        win = jnp.where((lane >= gi * pg) & (lane < (gi + 1) * pg), w, win)

    def seq_pos(r0, n):
        pos = base + r0 + lax.broadcasted_iota(jnp.int32, (n, 1), 0)
        pos = jnp.where(pos < 0, pos + seq_len, pos)
        return jnp.where(pos >= seq_len, pos - seq_len, pos)

    def project(r0, n):
        x = h_ref[r0:r0 + n, :]
        xn = (x * _rms_scale(x) * g_ref[...]).astype(_BF16)
        ua = jnp.dot(xn, wa_ref[...], preferred_element_type=_F32)
        cb_ref[r0:r0 + n, :] = ua[:, 0:cw]
        buf_ref[top + r0:top + r0 + n, 0:cw] = ua[:, cw:2 * cw] * ua[:, 2 * cw:3 * cw]
        buf_ref[top + r0:top + r0 + n, cw:cw + pw] = ua[:, 3 * cw:3 * cw + pw]
        uq = jnp.dot(xn, wqkv_ref[...], preferred_element_type=_F32)
        q_ref[r0:r0 + n, :] = (uq[:, 0:aw] * (HEAD_DIM ** -0.5 * LOG2_E)).astype(_BF16)
        k_ref[r0:r0 + n, :] = uq[:, aw:2 * aw].astype(_BF16)
        v_ref[r0:r0 + n, :] = uq[:, 2 * aw:3 * aw].astype(_BF16)

    def mix(r0, n):
        tpos = seq_pos(r0, n)
        lo = top + r0
        wc = wconv_ref[...]
        conv = wc[CONV_K - 1:CONV_K, :] * buf_ref[lo:lo + n, 0:cw]
        for s in range(1, CONV_K):
            shifted = buf_ref[lo - s:lo - s + n, 0:cw]
            conv = conv + jnp.where(tpos >= s, shifted, 0.0) * wc[CONV_K - 1 - s:CONV_K - s, :]
        ycp_ref[r0:r0 + n, 0:cw] = (cb_ref[r0:r0 + n, :] * conv).astype(_BF16)

        ext = n + HALO
        lo = HALO + r0
        epos = seq_pos(r0 - HALO, ext)
        cur = buf_ref[lo:lo + ext, cw:cw + pw]
        sums = cur
        shift = 1
        while shift < max(POOL_WINDOWS):
            if shift == 1:
                prev = buf_ref[lo - 1:lo - 1 + ext, cw:cw + pw]
            else:
                lvl_ref[lo:lo + ext, :] = cur
                prev = lvl_ref[lo - shift:lo - shift + ext, :]
            cur = cur + jnp.where(epos >= shift, prev, 0.0)
            shift *= 2
            sums = jnp.where(win >= shift, cur, sums)
        cnt = jnp.minimum(win, tpos + 1).astype(_F32)
        p_in = buf_ref[top + r0:top + r0 + n, cw:cw + pw]
        pooled = (sums[HALO:, :] / cnt - p_in).astype(_BF16)
        y_pool = jnp.dot(pooled, wpool_ref[...], preferred_element_type=_F32) * pscale_ref[...]
        ycp_ref[r0:r0 + n, cw:cw + pw] = y_pool.astype(_BF16)

    sub = tm // INPROJ_SPLIT
    for part in range(INPROJ_SPLIT):
        project(part * sub, sub)
    for part in range(INPROJ_SPLIT):
        mix(part * sub, sub)

    buf_ref[HALO:top, :] = buf_ref[tm + HALO:tm + top, :]


def _const_spec(shape):
    return pl.BlockSpec(shape, lambda *_: (0,) * len(shape), pipeline_mode=pl.Buffered(1))


def _inproj_call(h, g, wa, wqkv, wconv, wpool, pscale, *, seq_len):
    n_tok, d = h.shape
    cw = wconv.shape[1]
    pw = wpool.shape[0]
    aw = wqkv.shape[1] // 3
    tm = _token_tile(n_tok)
    assert tm <= seq_len and tm % (INPROJ_SPLIT * HALO) == 0
    kern = functools.partial(_inproj_kernel, tm=tm, seq_len=seq_len, cw=cw, pw=pw, aw=aw)
    row = lambda width: pl.BlockSpec((tm, width), lambda i: (i, 0))
    return pl.pallas_call(
        kern,
        grid=(n_tok // tm,),
        in_specs=[row(d), _const_spec(g.shape), _const_spec(wa.shape), _const_spec(wqkv.shape),
                  _const_spec(wconv.shape), _const_spec(wpool.shape), _const_spec(pscale.shape)],
        out_specs=[row(cw + pw), row(aw), row(aw), row(aw)],
        out_shape=[jax.ShapeDtypeStruct((n_tok, cw + pw), _BF16)]
        + [jax.ShapeDtypeStruct((n_tok, aw), _BF16)] * 3,
        scratch_shapes=[pltpu.VMEM((tm + 2 * HALO, cw + pw), _F32),
                        pltpu.VMEM((tm + 2 * HALO, pw), _F32),
                        pltpu.VMEM((tm, cw), _F32)],
        compiler_params=pltpu.CompilerParams(dimension_semantics=("arbitrary",),
                                             vmem_limit_bytes=VMEM_LIMIT),
        name="inproj_mix",
    )(h, g, wa, wqkv, wconv, wpool, pscale)


def _softplus2(t):
    sign_bit = jnp.uint32(0x80000000)
    neg_abs = lax.bitcast_convert_type(lax.bitcast_convert_type(t, jnp.uint32) | sign_bit, _F32)
    return jnp.maximum(t, 0.0) + jnp.log2(1.0 + jnp.exp2(neg_abs))


def _hi_lo(x):
    hi = x.astype(_BF16)
    lo = (x - hi.astype(_F32)).astype(_BF16)
    return jnp.concatenate([hi, lo], axis=1)


def _attn_kernel(q_ref, k_ref, v_ref, tri_ref, o_ref, qp_ref, kp_ref, vp_ref, op_ref, rs_ref,
                 *pipe_refs, seq_len):
    blk = ATT_BLK
    near = ATT_NEAR * blk
    front = near + blk - N_META
    n_pad_blk = (front + seq_len) // blk
    lane = lax.broadcasted_iota(jnp.int32, (1, 2 * HEAD_DIM), 1)
    head_lanes = [lane < HEAD_DIM, lane >= HEAD_DIM]
    causal = (lax.broadcasted_iota(jnp.int32, (blk, blk), 1)
              < lax.broadcasted_iota(jnp.int32, (blk, blk), 0))
    nt = (((1,), (1,)), ((), ()))

    for src, dst in ((q_ref, qp_ref), (k_ref, kp_ref), (v_ref, vp_ref)):
        dst[0:front, :] = jnp.zeros((front, 2 * HEAD_DIM), _BF16)
        dst[front:front + seq_len, :] = src[...]

    def by_head(x):
        return jnp.concatenate([jnp.where(m, x, jnp.zeros_like(x)) for m in head_lanes], axis=0)

    def logits(q_heads, k0, n_kb, diag):
        kb = kp_ref[pl.ds(k0, n_kb * blk), :]
        t = lax.dot_general(q_heads, kb, nt, preferred_element_type=_F32)
        if diag:
            last = jnp.where(jnp.concatenate([causal, causal], axis=0),
                             t[:, (n_kb - 1) * blk:], ATT_MASKED)
            t = last if n_kb == 1 else jnp.concatenate([t[:, :(n_kb - 1) * blk], last], axis=1)
        return t

    def cumsums(t_blk, n_kb):
        parts = [_hi_lo(_softplus2(t_blk(hd, b))) for hd in range(2) for b in range(n_kb)]
        return jnp.dot(jnp.concatenate(parts, axis=0), tri_ref[...], preferred_element_type=_F32)

    def weighted_values(t_blk, r_blk, k0, n_kb, offs):
        ws, new_offs = [], []
        for hd in range(2):
            off = None if offs is None else offs[hd]
            w_hd = [None] * n_kb
            for b in range(n_kb - 1, -1, -1):
                rb = r_blk(hd * n_kb + b)
                arg = t_blk(hd, b) - rb[:, 0:blk]
                if off is not None:
                    arg = arg - off
                w_hd[b] = jnp.exp2(arg)
                off = rb[:, blk:] if off is None else off + rb[:, blk:]
            ws += w_hd
            new_offs.append(off)
        a = jnp.concatenate(ws, axis=1).astype(_BF16)
        vb = vp_ref[pl.ds(k0, n_kb * blk), :]
        return jnp.dot(a, by_head(vb), preferred_element_type=_F32), new_offs

    def span(q_heads, k0, n_kb, diag, offs):
        t = logits(q_heads, k0, n_kb, diag)
        t_blk = lambda hd, b: t[hd * blk:(hd + 1) * blk, b * blk:(b + 1) * blk]
        r = cumsums(t_blk, n_kb)
        return weighted_values(t_blk, lambda n: r[n * blk:(n + 1) * blk, :], k0, n_kb, offs)

    n_kb = ATT_NEAR + 1
    first = ATT_NEAR
    last = n_pad_blk - 1
    t_refs, r_refs = pipe_refs[:ATT_PIPE], pipe_refs[ATT_PIPE:]

    def step_logits(i, slot):
        q0 = pl.multiple_of(i * blk, blk)
        k0 = pl.multiple_of(i * blk - near, blk)
        t_refs[slot][...] = logits(by_head(qp_ref[pl.ds(q0, blk), :]), k0, n_kb, True)

    def t_blk_of(slot):
        return lambda hd, b: t_refs[slot][hd * blk:(hd + 1) * blk, b * blk:(b + 1) * blk]

    def step_cumsums(slot):
        r_refs[slot][...] = cumsums(t_blk_of(slot), n_kb)

    def step_values(i, slot, gmin):
        q0 = pl.multiple_of(i * blk, blk)
        k0 = pl.multiple_of(i * blk - near, blk)
        acc, offs = weighted_values(t_blk_of(slot),
                                    lambda n: r_refs[slot][n * blk:(n + 1) * blk, :],
                                    k0, n_kb, None)
        rmin = jnp.minimum(offs[0], offs[1])
        op_ref[pl.ds(q0, blk), :] = acc.astype(_BF16)
        rs_ref[pl.ds(q0, blk), :] = rmin
        return jnp.minimum(gmin, rmin)

    def pipe_steps(g, gmin):
        for u in range(ATT_PIPE):
            i = first + g * ATT_PIPE + u
            step_logits(i + 2, (u + 2) % ATT_PIPE)
            step_cumsums((u + 1) % ATT_PIPE)
            gmin = step_values(i, u, gmin)
        return gmin

    n_steady = last - first - 1
    assert n_steady > 0 and n_steady % ATT_PIPE == 0
    step_logits(first, 0)
    step_logits(first + 1, 1)
    step_cumsums(0)
    gmin = lax.fori_loop(0, n_steady // ATT_PIPE, pipe_steps,
                         jnp.full((blk, blk), jnp.inf, _F32))
    step_cumsums((n_steady + 1) % ATT_PIPE)
    gmin = step_values(last - 1, n_steady % ATT_PIPE, gmin)
    gmin = step_values(last, (n_steady + 1) % ATT_PIPE, gmin)

    def full_pass(i, _):
        q0 = pl.multiple_of(i * blk, blk)

        @pl.when(jnp.min(rs_ref[pl.ds(q0, blk), :]) <= ATT_DEAD_LOG2)
        def _():
            q_heads = by_head(qp_ref[pl.ds(q0, blk), :])
            acc, offs = span(q_heads, q0, 1, True, None)

            def cond(c):
                return (c[0] >= first) & (c[1] > 0)

            def body(c):
                j, _, acc, off_a, off_b = c
                pv, (off_a, off_b) = span(q_heads, pl.multiple_of(j * blk, blk), 1, False,
                                          [off_a, off_b])
                alive = jnp.min(jnp.minimum(off_a, off_b)) <= ATT_DEAD_LOG2
                return j - 1, alive.astype(jnp.int32), acc + pv, off_a, off_b

            c = lax.while_loop(cond, body, (i - 1, jnp.int32(1), acc, offs[0], offs[1]))
            op_ref[pl.ds(q0, blk), :] = c[2].astype(_BF16)

        return 0

    @pl.when(jnp.min(gmin) <= ATT_DEAD_LOG2)
    def _():
        lax.fori_loop(first, n_pad_blk, full_pass, 0)

    o_ref[...] = op_ref[front:front + seq_len, :]


def _attn_call(q, k, v, tri):
    b, seq_len, aw = q.shape
    assert (seq_len - N_META) % ATT_BLK == 0
    pair = 2 * HEAD_DIM
    n_kb = ATT_NEAR + 1
    pad_len = n_kb * ATT_BLK - N_META + seq_len
    spec = pl.BlockSpec((None, seq_len, pair), lambda bi, pi: (bi, 0, pi))
    return pl.pallas_call(
        functools.partial(_attn_kernel, seq_len=seq_len),
        grid=(b, aw // pair),
        in_specs=[spec, spec, spec, _const_spec(tri.shape)],
        out_specs=spec,
        out_shape=jax.ShapeDtypeStruct((b, seq_len, aw), _BF16),
        scratch_shapes=[pltpu.VMEM((pad_len, pair), _BF16)] * 4
        + [pltpu.VMEM((pad_len, ATT_BLK), _F32)]
        + [pltpu.VMEM((2 * ATT_BLK, n_kb * ATT_BLK), _F32)] * ATT_PIPE
        + [pltpu.VMEM((2 * n_kb * ATT_BLK, 2 * ATT_BLK), _F32)] * ATT_PIPE,
        compiler_params=pltpu.CompilerParams(dimension_semantics=("parallel", "parallel"),
                                             vmem_limit_bytes=VMEM_LIMIT),
        name="stickbreak_attn",
    )(q, k, v, tri)


def _outmlp_kernel(h_ref, ycp_ref, ya_ref, wo_ref, g_ref, wup_ref, wdn_ref, gf_ref, o_ref,
                   *, ff_chunk, final_norm):
    n_cp = ycp_ref.shape[1]
    h1 = h_ref[...]
    h1 = h1 + jnp.dot(ycp_ref[...], wo_ref[0:n_cp, :], preferred_element_type=_F32)
    h1 = h1 + jnp.dot(ya_ref[...], wo_ref[n_cp:, :], preferred_element_type=_F32)
    xn = (h1 * _rms_scale(h1) * g_ref[...]).astype(_BF16)
    o_ref[...] = h1
    for c in range(wup_ref.shape[1] // ff_chunk):
        cols = slice(c * ff_chunk, (c + 1) * ff_chunk)
        m = jnp.dot(xn, wup_ref[:, cols], preferred_element_type=_F32)
        act = jnp.square(jnp.maximum(m, 0.0)).astype(_BF16)
        o_ref[...] += jnp.dot(act, wdn_ref[cols, :], preferred_element_type=_F32)
    if final_norm:
        out = o_ref[...]
        o_ref[...] = out * _rms_scale(out) * gf_ref[...]


def _outmlp_call(h, ycp, ya, wo, g, wup, wdn, gf, *, final_norm):
    n_tok, d = h.shape
    tm = _token_tile(n_tok)
    row = lambda width: pl.BlockSpec((tm, width), lambda i: (i, 0))
    kern = functools.partial(_outmlp_kernel, ff_chunk=512, final_norm=final_norm)
    return pl.pallas_call(
        kern,
        grid=(n_tok // tm,),
        in_specs=[row(d), row(ycp.shape[1]), row(ya.shape[1]), _const_spec(wo.shape),
                  _const_spec(g.shape), _const_spec(wup.shape), _const_spec(wdn.shape),
                  _const_spec(gf.shape)],
        out_specs=row(d),
        out_shape=jax.ShapeDtypeStruct((n_tok, d), _F32),
        compiler_params=pltpu.CompilerParams(dimension_semantics=("parallel",),
                                             vmem_limit_bytes=VMEM_LIMIT),
        name="outproj_mlp",
    )(h, ycp, ya, wo, g, wup, wdn, gf)


def _block_diag(w_grp):
    g, pg, _ = w_grp.shape
    out = jnp.zeros((g * pg, g * pg), w_grp.dtype)
    for gi in range(g):
        out = out.at[gi * pg:(gi + 1) * pg, gi * pg:(gi + 1) * pg].set(w_grp[gi])
    return out


def kernel(x, meta_tokens, g_mix, w_in, w_conv, w_pool, pool_scale, w_out, g_mlp, w_up, w_down, g_final):
    b, seq, d = x.shape
    depth = w_in.shape[0]
    cw = w_conv.shape[2]
    pw = pool_scale.shape[1]
    n_a = 3 * cw + pw
    meta = jnp.broadcast_to(meta_tokens[None].astype(x.dtype), (b, N_META, d))
    seq_len = N_META + seq
    h = jnp.concatenate([meta, x], axis=1).reshape(b * seq_len, d)

    j = lax.broadcasted_iota(jnp.int32, (2 * ATT_BLK, 2 * ATT_BLK), 0) % ATT_BLK
    s = lax.broadcasted_iota(jnp.int32, (2 * ATT_BLK, 2 * ATT_BLK), 1)
    tri = ((j >= s) | (s >= ATT_BLK)).astype(_BF16)

    for i in range(depth):
        wa = w_in[i, :, :n_a].astype(_BF16)
        wqkv = w_in[i, :, n_a:].astype(_BF16)
        ycp, q, k, v = _inproj_call(
            h, g_mix[i][None], wa, wqkv, w_conv[i], _block_diag(w_pool[i]).astype(_BF16),
            pool_scale[i][None], seq_len=seq_len)
        aw = q.shape[1]
        to_seq = lambda t: t.reshape(b, seq_len, aw)
        ya = _attn_call(to_seq(q), to_seq(k), to_seq(v), tri).reshape(b * seq_len, aw)
        h = _outmlp_call(h, ycp, ya, w_out[i].astype(_BF16), g_mlp[i][None],
                         w_up[i].astype(_BF16), w_down[i].astype(_BF16), g_final[None],
                         final_norm=(i == depth - 1))
    return h.reshape(b, seq_len, d)[:, N_META:]
```

```python
import functools

import jax
import jax.numpy as jnp
from jax import lax
from jax.experimental import pallas as pl
from jax.experimental.pallas import tpu as pltpu

N_META = 16
HEAD_DIM = 64
CONV_K = 3
POOL_WINDOWS = (2, 4, 8, 16)
EPS = 1e-6

ROW_ALIGN = 16
HALO = 16
TOKEN_TILE_CAP = 704
X_TILE = 512
FF_CHUNK = 512
INPROJ_SPLIT = 2
ATT_BLK = 128
ATT_NEAR = 2
ATT_PIPE = 3
ATT_DEAD_LOG2 = 152.0
ATT_MASKED = -1e30
LOG2_E = 1.4426950408889634
VMEM_LIMIT = 56 * 1024 * 1024

_BF16 = jnp.bfloat16
_F32 = jnp.float32


def _seq_tile(seq_len):
    best = None
    for t in range(ROW_ALIGN, TOKEN_TILE_CAP + 1, ROW_ALIGN):
        if seq_len % t == 0:
            best = t
    assert best is not None, seq_len
    return best


def _row_groups(tm, parts):
    units = tm // ROW_ALIGN
    sizes = [(units // parts + (p < units % parts)) * ROW_ALIGN for p in range(parts)]
    starts = [sum(sizes[:p]) for p in range(parts)]
    return [(s, n) for s, n in zip(starts, sizes) if n]


def _rms_scale(x):
    return lax.rsqrt(jnp.mean(x * x, axis=-1, keepdims=True) + EPS)


def _tile_rows(src, r0, n, first):
    if len(src) == 1:
        return src[0][r0:r0 + n, :]
    x_ref, meta_ref = src
    plain = x_ref[r0:r0 + n, :]
    if r0 == 0:
        lead = jnp.concatenate([meta_ref[...], x_ref[0:n - N_META, :]], axis=0)
    else:
        lead = x_ref[r0 - N_META:r0 - N_META + n, :]
    return jnp.where(first, lead, plain)


def _inproj_kernel(*refs, n_src, tm, seq_len, cw, pw, aw):
    src = refs[:n_src]
    (g_ref, wa_ref, wqkv_ref, wconv_ref, wpool_ref, pscale_ref,
     ycp_ref, q_ref, k_ref, v_ref, buf_ref, lvl_ref, cb_ref) = refs[n_src:]
    i = pl.program_id(0)
    first = lax.rem(i, seq_len // tm) == 0
    top = 2 * HALO

    @pl.when(i == 0)
    def _():
        buf_ref[0:top, :] = jnp.zeros((top, cw + pw), _F32)
        lvl_ref[0:top, :] = jnp.zeros((top, pw), _F32)

    base = lax.rem(i * tm, seq_len)
    lane = lax.broadcasted_iota(jnp.int32, (1, pw), 1)
    pg = pw // len(POOL_WINDOWS)
    win = jnp.zeros((1, pw), jnp.int32)
    for gi, w in enumerate(POOL_WINDOWS):
        win = jnp.where((lane >= gi * pg) & (lane < (gi + 1) * pg), w, win)

    def seq_pos(r0, n):
        pos = base + r0 + lax.broadcasted_iota(jnp.int32, (n, 1), 0)
        pos = jnp.where(pos < 0, pos + seq_len, pos)
        return jnp.where(pos >= seq_len, pos - seq_len, pos)

    def project(r0, n):
        x = _tile_rows(src, r0, n, first)
        xn = (x * _rms_scale(x) * g_ref[...]).astype(_BF16)
        ua = jnp.dot(xn, wa_ref[...], preferred_element_type=_F32)
        cb_ref[r0:r0 + n, :] = ua[:, 0:cw]
        buf_ref[top + r0:top + r0 + n, 0:cw] = ua[:, cw:2 * cw] * ua[:, 2 * cw:3 * cw]
        buf_ref[top + r0:top + r0 + n, cw:cw + pw] = ua[:, 3 * cw:3 * cw + pw]
        uq = jnp.dot(xn, wqkv_ref[...], preferred_element_type=_F32)
        q_ref[r0:r0 + n, :] = (uq[:, 0:aw] * (HEAD_DIM ** -0.5 * LOG2_E)).astype(_BF16)
        k_ref[r0:r0 + n, :] = uq[:, aw:2 * aw].astype(_BF16)
        v_ref[r0:r0 + n, :] = uq[:, 2 * aw:3 * aw].astype(_BF16)

    def mix(r0, n):
        tpos = seq_pos(r0, n)
        lo = top + r0
        wc = wconv_ref[...]
        conv = wc[CONV_K - 1:CONV_K, :] * buf_ref[lo:lo + n, 0:cw]
        for s in range(1, CONV_K):
            shifted = buf_ref[lo - s:lo - s + n, 0:cw]
            conv = conv + jnp.where(tpos >= s, shifted, 0.0) * wc[CONV_K - 1 - s:CONV_K - s, :]
        ycp_ref[r0:r0 + n, 0:cw] = (cb_ref[r0:r0 + n, :] * conv).astype(_BF16)

        ext = n + HALO
        lo = HALO + r0
        epos = seq_pos(r0 - HALO, ext)
        cur = buf_ref[lo:lo + ext, cw:cw + pw]
        sums = cur
        shift = 1
        while shift < max(POOL_WINDOWS):
            if shift == 1:
                prev = buf_ref[lo - 1:lo - 1 + ext, cw:cw + pw]
            else:
                lvl_ref[lo:lo + ext, :] = cur
                prev = lvl_ref[lo - shift:lo - shift + ext, :]
            cur = cur + jnp.where(epos >= shift, prev, 0.0)
            shift *= 2
            sums = jnp.where(win >= shift, cur, sums)
        cnt = jnp.minimum(win, tpos + 1).astype(_F32)
        p_in = buf_ref[top + r0:top + r0 + n, cw:cw + pw]
        pooled = (sums[HALO:, :] / cnt - p_in).astype(_BF16)
        y_pool = jnp.dot(pooled, wpool_ref[...], preferred_element_type=_F32) * pscale_ref[...]
        ycp_ref[r0:r0 + n, cw:cw + pw] = y_pool.astype(_BF16)

    groups = _row_groups(tm, INPROJ_SPLIT)
    for r0, n in groups:
        project(r0, n)
    for r0, n in groups:
        mix(r0, n)

    buf_ref[HALO:top, :] = buf_ref[tm + HALO:tm + top, :]


def _const_spec(shape):
    return pl.BlockSpec(shape, lambda *_: (0,) * len(shape), pipeline_mode=pl.Buffered(1))


def _src_specs(src, tm, seq_len):
    if len(src) == 1:
        (h,) = src
        return [h], [pl.BlockSpec((tm, h.shape[1]), lambda i: (i, 0))]
    x, meta = src
    b, seq, d = x.shape
    tiles_per_seq = seq_len // tm
    assert seq % ROW_ALIGN == 0 and N_META % ROW_ALIGN == 0
    x_rows = lambda i: (pl.multiple_of(
        (i // tiles_per_seq) * seq + jnp.maximum(lax.rem(i, tiles_per_seq) * tm - N_META, 0),
        ROW_ALIGN), 0)
    return [x.reshape(b * seq, d), meta], [pl.BlockSpec((pl.Element(tm), pl.Element(d)), x_rows),
                                           _const_spec(meta.shape)]


def _inproj_call(src, g, wa, wqkv, wconv, wpool, pscale, *, n_tok, seq_len):
    cw = wconv.shape[1]
    pw = wpool.shape[0]
    aw = wqkv.shape[1] // 3
    tm = _seq_tile(seq_len)
    src_arrays, src_specs = _src_specs(src, tm, seq_len)
    kern = functools.partial(_inproj_kernel, n_src=len(src), tm=tm, seq_len=seq_len,
                             cw=cw, pw=pw, aw=aw)
    row = lambda width: pl.BlockSpec((tm, width), lambda i: (i, 0))
    return pl.pallas_call(
        kern,
        grid=(n_tok // tm,),
        in_specs=src_specs + [_const_spec(g.shape), _const_spec(wa.shape), _const_spec(wqkv.shape),
                              _const_spec(wconv.shape), _const_spec(wpool.shape),
                              _const_spec(pscale.shape)],
        out_specs=[row(cw + pw), row(aw), row(aw), row(aw)],
        out_shape=[jax.ShapeDtypeStruct((n_tok, cw + pw), _BF16)]
        + [jax.ShapeDtypeStruct((n_tok, aw), _BF16)] * 3,
        scratch_shapes=[pltpu.VMEM((tm + 2 * HALO, cw + pw), _F32),
                        pltpu.VMEM((tm + 2 * HALO, pw), _F32),
                        pltpu.VMEM((tm, cw), _F32)],
        compiler_params=pltpu.CompilerParams(dimension_semantics=("arbitrary",),
                                             vmem_limit_bytes=VMEM_LIMIT),
        name="inproj_mix",
    )(*src_arrays, g, wa, wqkv, wconv, wpool, pscale)


def _softplus2(t):
    return jnp.maximum(t, 0.0) + jnp.log2(1.0 + jnp.exp2(-jnp.abs(t)))


def _hi_lo(x):
    hi = x.astype(_BF16)
    lo = (x - hi.astype(_F32)).astype(_BF16)
    return jnp.concatenate([hi, lo], axis=1)


def _attn_kernel(q_ref, k_ref, v_ref, tri_ref, o_ref, qp_ref, kp_ref, vp_ref, op_ref, dead_ref,
                 *pipe_refs, seq_len):
    blk = ATT_BLK
    pair = 2 * HEAD_DIM
    n_pairs = q_ref.shape[1] // pair
    near = ATT_NEAR * blk
    front = near + blk - N_META
    n_kb = ATT_NEAR + 1
    first = ATT_NEAR
    n_qb = (front + seq_len) // blk - first
    n_all = n_pairs * n_qb
    lane = lax.broadcasted_iota(jnp.int32, (1, pair), 1)
    head_lanes = [lane < HEAD_DIM, lane >= HEAD_DIM]
    causal = (lax.broadcasted_iota(jnp.int32, (blk, blk), 1)
              < lax.broadcasted_iota(jnp.int32, (blk, blk), 0))
    nt = (((1,), (1,)), ((), ()))

    for src, dst in ((q_ref, qp_ref), (k_ref, kp_ref), (v_ref, vp_ref)):
        for p in range(n_pairs):
            dst[p, 0:front, :] = jnp.zeros((front, pair), _BF16)
            dst[p, front:front + seq_len, :] = src[:, p * pair:(p + 1) * pair]

    def by_head(x):
        return jnp.concatenate([jnp.where(m, x, jnp.zeros_like(x)) for m in head_lanes], axis=0)

    def where(n):
        p = n // n_qb
        return p, pl.multiple_of((first + n - p * n_qb) * blk, blk)

    def logits(q_heads, p, k0, n_kb, diag):
        kb = kp_ref[p, pl.ds(k0, n_kb * blk), :]
        t = lax.dot_general(q_heads, kb, nt, preferred_element_type=_F32)
        if diag:
            last = jnp.where(jnp.concatenate([causal, causal], axis=0),
                             t[:, (n_kb - 1) * blk:], ATT_MASKED)
            t = last if n_kb == 1 else jnp.concatenate([t[:, :(n_kb - 1) * blk], last], axis=1)
        return t

    def cumsums(t_blk, n_kb):
        parts = [_hi_lo(_softplus2(t_blk(hd, b))) for hd in range(2) for b in range(n_kb)]
        return jnp.dot(jnp.concatenate(parts, axis=0), tri_ref[...], preferred_element_type=_F32)

    def weighted_values(t_blk, r_blk, p, k0, n_kb, offs):
        ws, new_offs = [], []
        for hd in range(2):
            off = None if offs is None else offs[hd]
            w_hd = [None] * n_kb
            for b in range(n_kb - 1, -1, -1):
                rb = r_blk(hd * n_kb + b)
                arg = t_blk(hd, b) - rb[:, 0:blk]
                if off is not None:
                    arg = arg - off
                w_hd[b] = jnp.exp2(arg)
                off = rb[:, blk:] if off is None else off + rb[:, blk:]
            ws += w_hd
            new_offs.append(off)
        a = jnp.concatenate(ws, axis=1).astype(_BF16)
        vb = vp_ref[p, pl.ds(k0, n_kb * blk), :]
        return jnp.dot(a, by_head(vb), preferred_element_type=_F32), new_offs

    def span(q_heads, p, k0, n_kb, diag, offs):
        t = logits(q_heads, p, k0, n_kb, diag)
        t_blk = lambda hd, b: t[hd * blk:(hd + 1) * blk, b * blk:(b + 1) * blk]
        r = cumsums(t_blk, n_kb)
        return weighted_values(t_blk, lambda n: r[n * blk:(n + 1) * blk, :], p, k0, n_kb, offs)

    t_refs, r_refs = pipe_refs[:ATT_PIPE], pipe_refs[ATT_PIPE:]

    def step_logits(n, slot):
        p, q0 = where(n)
        t_refs[slot][...] = logits(by_head(qp_ref[p, pl.ds(q0, blk), :]), p, q0 - near, n_kb, True)

    def t_blk_of(slot):
        return lambda hd, b: t_refs[slot][hd * blk:(hd + 1) * blk, b * blk:(b + 1) * blk]

    def step_cumsums(slot):
        r_refs[slot][...] = cumsums(t_blk_of(slot), n_kb)

    def step_values(n, slot):
        p, q0 = where(n)
        acc, offs = weighted_values(t_blk_of(slot),
                                    lambda m: r_refs[slot][m * blk:(m + 1) * blk, :],
                                    p, q0 - near, n_kb, None)
        op_ref[p, pl.ds(q0, blk), :] = acc.astype(_BF16)
        dead_ref[n] = jnp.min(jnp.minimum(offs[0], offs[1]))

    def pipe_steps(g, _):
        for u in range(ATT_PIPE):
            n = g * ATT_PIPE + u
            step_logits(n + 2, (u + 2) % ATT_PIPE)
            step_cumsums((u + 1) % ATT_PIPE)
            step_values(n, u)
        return 0

    n_steady = n_all - 2
    assert n_steady > 0 and n_steady % ATT_PIPE == 0
    step_logits(0, 0)
    step_logits(1, 1)
    step_cumsums(0)
    lax.fori_loop(0, n_steady // ATT_PIPE, pipe_steps, 0)
    step_cumsums((n_steady + 1) % ATT_PIPE)
    step_values(n_all - 2, n_steady % ATT_PIPE)
    step_values(n_all - 1, (n_steady + 1) % ATT_PIPE)

    def full_pass(n, _):
        @pl.when(dead_ref[n] <= ATT_DEAD_LOG2)
        def _():
            p, q0 = where(n)
            q_heads = by_head(qp_ref[p, pl.ds(q0, blk), :])
            acc, offs = span(q_heads, p, q0, 1, True, None)

            def cond(c):
                return (c[0] >= first * blk) & (c[1] > 0)

            def body(c):
                k0, _, acc, off_a, off_b = c
                pv, (off_a, off_b) = span(q_heads, p, pl.multiple_of(k0, blk), 1, False,
                                          [off_a, off_b])
                alive = jnp.min(jnp.minimum(off_a, off_b)) <= ATT_DEAD_LOG2
                return k0 - blk, alive.astype(jnp.int32), acc + pv, off_a, off_b

            c = lax.while_loop(cond, body, (q0 - blk, jnp.int32(1), acc, offs[0], offs[1]))
            op_ref[p, pl.ds(q0, blk), :] = c[2].astype(_BF16)

        return 0

    lax.fori_loop(0, n_all, full_pass, 0)

    for p in range(n_pairs):
        o_ref[:, p * pair:(p + 1) * pair] = op_ref[p, front:front + seq_len, :]


def _attn_call(q, k, v, tri):
    b, seq_len, aw = q.shape
    pair = 2 * HEAD_DIM
    n_kb = ATT_NEAR + 1
    pad_len = n_kb * ATT_BLK - N_META + seq_len
    assert pad_len % ATT_BLK == 0 and aw % pair == 0
    n_all = (aw // pair) * (pad_len // ATT_BLK - ATT_NEAR)
    spec = pl.BlockSpec((None, seq_len, aw), lambda bi: (bi, 0, 0))
    return pl.pallas_call(
        functools.partial(_attn_kernel, seq_len=seq_len),
        grid=(b,),
        in_specs=[spec, spec, spec, _const_spec(tri.shape)],
        out_specs=spec,
        out_shape=jax.ShapeDtypeStruct((b, seq_len, aw), _BF16),
        scratch_shapes=[pltpu.VMEM((aw // pair, pad_len, pair), _BF16)] * 4
        + [pltpu.SMEM((n_all,), _F32)]
        + [pltpu.VMEM((2 * ATT_BLK, n_kb * ATT_BLK), _F32)] * ATT_PIPE
        + [pltpu.VMEM((2 * n_kb * ATT_BLK, 2 * ATT_BLK), _F32)] * ATT_PIPE,
        compiler_params=pltpu.CompilerParams(dimension_semantics=("parallel",),
                                             vmem_limit_bytes=VMEM_LIMIT),
        name="stickbreak_attn",
    )(q, k, v, tri)


def _outmlp_kernel(*refs, n_src, tiles_per_seq, ff_chunk, final_norm):
    src = refs[:n_src]
    ycp_ref, ya_ref, wo_ref, g_ref, wup_ref, wdn_ref = refs[n_src:n_src + 6]
    o_ref = refs[-1]
    n_cp = ycp_ref.shape[1]
    first = lax.rem(pl.program_id(0), tiles_per_seq) == 0
    h1 = _tile_rows(src, 0, o_ref.shape[0], first)
    h1 = h1 + jnp.dot(ycp_ref[...], wo_ref[0:n_cp, :], preferred_element_type=_F32)
    h1 = h1 + jnp.dot(ya_ref[...], wo_ref[n_cp:, :], preferred_element_type=_F32)
    xn = (h1 * _rms_scale(h1) * g_ref[...]).astype(_BF16)
    o_ref[...] = h1
    for c in range(wup_ref.shape[1] // ff_chunk):
        cols = slice(c * ff_chunk, (c + 1) * ff_chunk)
        m = jnp.dot(xn, wup_ref[:, cols], preferred_element_type=_F32)
        act = jnp.square(jnp.maximum(m, 0.0)).astype(_BF16)
        o_ref[...] += jnp.dot(act, wdn_ref[cols, :], preferred_element_type=_F32)
    if final_norm:
        gf_ref = refs[n_src + 6]
        out = o_ref[...]
        o_ref[...] = out * _rms_scale(out) * gf_ref[...]


def _outmlp_call(src, ycp, ya, wo, g, wup, wdn, *, n_tok, seq_len, g_final=None):
    d = wo.shape[1]
    weights = [wo, g, wup, wdn] + ([] if g_final is None else [g_final])
    if g_final is None:
        tm = _seq_tile(seq_len)
        src_arrays, src_specs = _src_specs(src, tm, seq_len)
        row = lambda width: pl.BlockSpec((tm, width), lambda i: (i, 0))
        kern = functools.partial(_outmlp_kernel, n_src=len(src), tiles_per_seq=seq_len // tm,
                                 ff_chunk=FF_CHUNK, final_norm=False)
        return pl.pallas_call(
            kern,
            grid=(n_tok // tm,),
            in_specs=src_specs + [row(ycp.shape[1]), row(ya.shape[1])]
            + [_const_spec(w.shape) for w in weights],
            out_specs=row(d),
            out_shape=jax.ShapeDtypeStruct((n_tok, d), _F32),
            compiler_params=pltpu.CompilerParams(dimension_semantics=("parallel",),
                                                 vmem_limit_bytes=VMEM_LIMIT),
            name="outproj_mlp",
        )(*src_arrays, ycp, ya, *weights)

    (h,) = src
    b, seq = n_tok // seq_len, seq_len - N_META
    tm = X_TILE
    assert seq % tm == 0
    x_rows = lambda width: pl.BlockSpec((pl.Element(tm), pl.Element(width)),
                                        lambda bi, j: (pl.multiple_of(
                                            bi * seq_len + N_META + j * tm, ROW_ALIGN), 0))
    kern = functools.partial(_outmlp_kernel, n_src=1, tiles_per_seq=1, ff_chunk=FF_CHUNK,
                             final_norm=True)
    return pl.pallas_call(
        kern,
        grid=(b, seq // tm),
        in_specs=[x_rows(d), x_rows(ycp.shape[1]), x_rows(ya.shape[1])]
        + [_const_spec(w.shape) for w in weights],
        out_specs=pl.BlockSpec((None, tm, d), lambda bi, j: (bi, j, 0)),
        out_shape=jax.ShapeDtypeStruct((b, seq, d), _F32),
        compiler_params=pltpu.CompilerParams(dimension_semantics=("parallel", "parallel"),
                                             vmem_limit_bytes=VMEM_LIMIT),
        name="outproj_mlp_final",
    )(h, ycp, ya, *weights)


def _block_diag(w_grp):
    g, pg, _ = w_grp.shape
    out = jnp.zeros((g * pg, g * pg), w_grp.dtype)
    for gi in range(g):
        out = out.at[gi * pg:(gi + 1) * pg, gi * pg:(gi + 1) * pg].set(w_grp[gi])
    return out


def kernel(x, meta_tokens, g_mix, w_in, w_conv, w_pool, pool_scale, w_out, g_mlp, w_up, w_down, g_final):
    b, seq, d = x.shape
    depth = w_in.shape[0]
    assert depth >= 2
    cw = w_conv.shape[2]
    pw = pool_scale.shape[1]
    n_a = 3 * cw + pw
    seq_len = N_META + seq
    n_tok = b * seq_len

    j = lax.broadcasted_iota(jnp.int32, (2 * ATT_BLK, 2 * ATT_BLK), 0) % ATT_BLK
    s = lax.broadcasted_iota(jnp.int32, (2 * ATT_BLK, 2 * ATT_BLK), 1)
    tri = ((j >= s) | (s >= ATT_BLK)).astype(_BF16)

    src = (x, meta_tokens.astype(x.dtype))
    for i in range(depth):
        wa = w_in[i, :, :n_a].astype(_BF16)
        wqkv = w_in[i, :, n_a:].astype(_BF16)
        ycp, q, k, v = _inproj_call(
            src, g_mix[i][None], wa, wqkv, w_conv[i], _block_diag(w_pool[i]).astype(_BF16),
            pool_scale[i][None], n_tok=n_tok, seq_len=seq_len)
        aw = q.shape[1]
        to_seq = lambda t: t.reshape(b, seq_len, aw)
        ya = _attn_call(to_seq(q), to_seq(k), to_seq(v), tri).reshape(n_tok, aw)
        h = _outmlp_call(src, ycp, ya, w_out[i].astype(_BF16), g_mlp[i][None],
                         w_up[i].astype(_BF16), w_down[i].astype(_BF16), n_tok=n_tok,
                         seq_len=seq_len, g_final=g_final[None] if i == depth - 1 else None)
        src = (h,)
    return h
```

```python
import functools

import jax
import jax.numpy as jnp
from jax import lax
from jax.experimental import pallas as pl
from jax.experimental.pallas import tpu as pltpu

N_META = 16
HEAD_DIM = 64
CONV_K = 3
POOL_WINDOWS = (2, 4, 8, 16)
EPS = 1e-6

ROW_ALIGN = 16
HALO = 16
TOKEN_TILE_CAP = 704
X_TILE = 512
FF_CHUNK = 512
INPROJ_SPLIT = 2
ATT_BLK = 128
ATT_NEAR = 2
ATT_PIPE = 3
ATT_MAX_TRIP_ROUNDS = 11
ATT_DEAD_LOG2 = 152.0
ATT_MASKED = -1e30
LOG2_E = 1.4426950408889634
VMEM_LIMIT = 56 * 1024 * 1024

_BF16 = jnp.bfloat16
_F32 = jnp.float32


def _seq_tile(seq_len):
    best = None
    for t in range(ROW_ALIGN, TOKEN_TILE_CAP + 1, ROW_ALIGN):
        if seq_len % t == 0:
            best = t
    assert best is not None, seq_len
    return best


def _row_groups(tm, parts):
    units = tm // ROW_ALIGN
    sizes = [(units // parts + (p < units % parts)) * ROW_ALIGN for p in range(parts)]
    starts = [sum(sizes[:p]) for p in range(parts)]
    return [(s, n) for s, n in zip(starts, sizes) if n]


def _rms_scale(x):
    return lax.rsqrt(jnp.mean(x * x, axis=-1, keepdims=True) + EPS)


def _tile_rows(src, r0, n, first):
    if len(src) == 1:
        return src[0][r0:r0 + n, :]
    x_ref, meta_ref = src
    plain = x_ref[r0:r0 + n, :]
    if r0 == 0:
        lead = jnp.concatenate([meta_ref[...], x_ref[0:n - N_META, :]], axis=0)
    else:
        lead = x_ref[r0 - N_META:r0 - N_META + n, :]
    return jnp.where(first, lead, plain)


def _inproj_kernel(*refs, n_src, tm, seq_len, cw, pw, aw):
    src = refs[:n_src]
    (g_ref, wa_ref, wqkv_ref, wconv_ref, wpool_ref, pscale_ref,
     ycp_ref, q_ref, k_ref, v_ref, buf_ref, lvl_ref, cb_ref) = refs[n_src:]
    i = pl.program_id(0)
    first = lax.rem(i, seq_len // tm) == 0
    top = 2 * HALO

    @pl.when(i == 0)
    def _():
        buf_ref[0:top, :] = jnp.zeros((top, cw + pw), _F32)
        lvl_ref[0:top, :] = jnp.zeros((top, pw), _F32)

    base = lax.rem(i * tm, seq_len)
    lane = lax.broadcasted_iota(jnp.int32, (1, pw), 1)
    pg = pw // len(POOL_WINDOWS)
    win = jnp.zeros((1, pw), jnp.int32)
    for gi, w in enumerate(POOL_WINDOWS):
        win = jnp.where((lane >= gi * pg) & (lane < (gi + 1) * pg), w, win)

    def seq_pos(r0, n):
        pos = base + r0 + lax.broadcasted_iota(jnp.int32, (n, 1), 0)
        pos = jnp.where(pos < 0, pos + seq_len, pos)
        return jnp.where(pos >= seq_len, pos - seq_len, pos)

    def project(r0, n):
        x = _tile_rows(src, r0, n, first)
        xn = (x * _rms_scale(x) * g_ref[...]).astype(_BF16)
        ua = jnp.dot(xn, wa_ref[...], preferred_element_type=_F32)
        cb_ref[r0:r0 + n, :] = ua[:, 0:cw]
        buf_ref[top + r0:top + r0 + n, 0:cw] = ua[:, cw:2 * cw] * ua[:, 2 * cw:3 * cw]
        buf_ref[top + r0:top + r0 + n, cw:cw + pw] = ua[:, 3 * cw:3 * cw + pw]
        uq = jnp.dot(xn, wqkv_ref[...], preferred_element_type=_F32)
        q_ref[r0:r0 + n, :] = (uq[:, 0:aw] * (HEAD_DIM ** -0.5 * LOG2_E)).astype(_BF16)
        k_ref[r0:r0 + n, :] = uq[:, aw:2 * aw].astype(_BF16)
        v_ref[r0:r0 + n, :] = uq[:, 2 * aw:3 * aw].astype(_BF16)

    def mix(r0, n):
        tpos = seq_pos(r0, n)
        lo = top + r0
        wc = wconv_ref[...]
        conv = wc[CONV_K - 1:CONV_K, :] * buf_ref[lo:lo + n, 0:cw]
        for s in range(1, CONV_K):
            shifted = buf_ref[lo - s:lo - s + n, 0:cw]
            conv = conv + jnp.where(tpos >= s, shifted, 0.0) * wc[CONV_K - 1 - s:CONV_K - s, :]
        ycp_ref[r0:r0 + n, 0:cw] = (cb_ref[r0:r0 + n, :] * conv).astype(_BF16)

        ext = n + HALO
        lo = HALO + r0
        epos = seq_pos(r0 - HALO, ext)
        cur = buf_ref[lo:lo + ext, cw:cw + pw]
        sums = cur
        shift = 1
        while shift < max(POOL_WINDOWS):
            if shift == 1:
                prev = buf_ref[lo - 1:lo - 1 + ext, cw:cw + pw]
            else:
                lvl_ref[lo:lo + ext, :] = cur
                prev = lvl_ref[lo - shift:lo - shift + ext, :]
            cur = cur + jnp.where(epos >= shift, prev, 0.0)
            shift *= 2
            sums = jnp.where(win >= shift, cur, sums)
        cnt = jnp.minimum(win, tpos + 1).astype(_F32)
        p_in = buf_ref[top + r0:top + r0 + n, cw:cw + pw]
        pooled = (sums[HALO:, :] / cnt - p_in).astype(_BF16)
        y_pool = jnp.dot(pooled, wpool_ref[...], preferred_element_type=_F32) * pscale_ref[...]
        ycp_ref[r0:r0 + n, cw:cw + pw] = y_pool.astype(_BF16)

    groups = _row_groups(tm, INPROJ_SPLIT)
    for r0, n in groups:
        project(r0, n)
    for r0, n in groups:
        mix(r0, n)

    buf_ref[HALO:top, :] = buf_ref[tm + HALO:tm + top, :]


def _const_spec(shape):
    return pl.BlockSpec(shape, lambda *_: (0,) * len(shape), pipeline_mode=pl.Buffered(1))


def _src_specs(src, tm, seq_len):
    if len(src) == 1:
        (h,) = src
        return [h], [pl.BlockSpec((tm, h.shape[1]), lambda i: (i, 0))]
    x, meta = src
    b, seq, d = x.shape
    tiles_per_seq = seq_len // tm
    assert seq % ROW_ALIGN == 0 and N_META % ROW_ALIGN == 0
    x_rows = lambda i: (pl.multiple_of(
        (i // tiles_per_seq) * seq + jnp.maximum(lax.rem(i, tiles_per_seq) * tm - N_META, 0),
        ROW_ALIGN), 0)
    return [x.reshape(b * seq, d), meta], [pl.BlockSpec((pl.Element(tm), pl.Element(d)), x_rows),
                                           _const_spec(meta.shape)]


def _inproj_call(src, g, wa, wqkv, wconv, wpool, pscale, *, n_tok, seq_len):
    cw = wconv.shape[1]
    pw = wpool.shape[0]
    aw = wqkv.shape[1] // 3
    tm = _seq_tile(seq_len)
    src_arrays, src_specs = _src_specs(src, tm, seq_len)
    kern = functools.partial(_inproj_kernel, n_src=len(src), tm=tm, seq_len=seq_len,
                             cw=cw, pw=pw, aw=aw)
    row = lambda width: pl.BlockSpec((tm, width), lambda i: (i, 0))
    return pl.pallas_call(
        kern,
        grid=(n_tok // tm,),
        in_specs=src_specs + [_const_spec(g.shape), _const_spec(wa.shape), _const_spec(wqkv.shape),
                              _const_spec(wconv.shape), _const_spec(wpool.shape),
                              _const_spec(pscale.shape)],
        out_specs=[row(cw + pw), row(aw), row(aw), row(aw)],
        out_shape=[jax.ShapeDtypeStruct((n_tok, cw + pw), _BF16)]
        + [jax.ShapeDtypeStruct((n_tok, aw), _BF16)] * 3,
        scratch_shapes=[pltpu.VMEM((tm + 2 * HALO, cw + pw), _F32),
                        pltpu.VMEM((tm + 2 * HALO, pw), _F32),
                        pltpu.VMEM((tm, cw), _F32)],
        compiler_params=pltpu.CompilerParams(dimension_semantics=("arbitrary",),
                                             vmem_limit_bytes=VMEM_LIMIT),
        name="inproj_mix",
    )(*src_arrays, g, wa, wqkv, wconv, wpool, pscale)


def _softplus2(t):
    return jnp.maximum(t, 0.0) + jnp.log2(1.0 + jnp.exp2(-jnp.abs(t)))


def _hi_lo(x):
    hi = x.astype(_BF16)
    lo = (x - hi.astype(_F32)).astype(_BF16)
    return jnp.concatenate([hi, lo], axis=1)


def _attn_kernel(q_ref, k_ref, v_ref, tri_ref, o_ref, qp_ref, kp_ref, vp_ref, op_ref, dead_ref,
                 *pipe_refs, seq_len):
    blk = ATT_BLK
    pair = 2 * HEAD_DIM
    n_pairs = q_ref.shape[1] // pair
    near = ATT_NEAR * blk
    front = near + blk - N_META
    n_kb = ATT_NEAR + 1
    first = ATT_NEAR
    n_qb = (front + seq_len) // blk - first
    n_all = n_pairs * n_qb
    lane = lax.broadcasted_iota(jnp.int32, (1, pair), 1)
    head_lanes = [lane < HEAD_DIM, lane >= HEAD_DIM]
    causal = (lax.broadcasted_iota(jnp.int32, (blk, blk), 1)
              < lax.broadcasted_iota(jnp.int32, (blk, blk), 0))
    nt = (((1,), (1,)), ((), ()))

    for src, dst in ((q_ref, qp_ref), (k_ref, kp_ref), (v_ref, vp_ref)):
        for p in range(n_pairs):
            dst[p, 0:front, :] = jnp.zeros((front, pair), _BF16)
            dst[p, front:front + seq_len, :] = src[:, p * pair:(p + 1) * pair]

    def by_head(x):
        return jnp.concatenate([jnp.where(m, x, jnp.zeros_like(x)) for m in head_lanes], axis=0)

    def where(n):
        p = n // n_qb
        return p, pl.multiple_of((first + n - p * n_qb) * blk, blk)

    def logits(q_heads, p, k0, n_kb, diag):
        kb = kp_ref[p, pl.ds(k0, n_kb * blk), :]
        t = lax.dot_general(q_heads, kb, nt, preferred_element_type=_F32)
        if diag:
            last = jnp.where(jnp.concatenate([causal, causal], axis=0),
                             t[:, (n_kb - 1) * blk:], ATT_MASKED)
            t = last if n_kb == 1 else jnp.concatenate([t[:, :(n_kb - 1) * blk], last], axis=1)
        return t

    def cumsums(t_blk, n_kb):
        parts = [_hi_lo(_softplus2(t_blk(hd, b))) for hd in range(2) for b in range(n_kb)]
        return jnp.dot(jnp.concatenate(parts, axis=0), tri_ref[...], preferred_element_type=_F32)

    def weighted_values(t_blk, r_blk, p, k0, n_kb, offs):
        ws, new_offs = [], []
        for hd in range(2):
            off = None if offs is None else offs[hd]
            w_hd = [None] * n_kb
            for b in range(n_kb - 1, -1, -1):
                rb = r_blk(hd * n_kb + b)
                arg = t_blk(hd, b) - rb[:, 0:blk]
                if off is not None:
                    arg = arg - off
                w_hd[b] = jnp.exp2(arg)
                off = rb[:, blk:] if off is None else off + rb[:, blk:]
            ws += w_hd
            new_offs.append(off)
        a = jnp.concatenate(ws, axis=1).astype(_BF16)
        vb = vp_ref[p, pl.ds(k0, n_kb * blk), :]
        return jnp.dot(a, by_head(vb), preferred_element_type=_F32), new_offs

    def span(q_heads, p, k0, n_kb, diag, offs):
        t = logits(q_heads, p, k0, n_kb, diag)
        t_blk = lambda hd, b: t[hd * blk:(hd + 1) * blk, b * blk:(b + 1) * blk]
        r = cumsums(t_blk, n_kb)
        return weighted_values(t_blk, lambda n: r[n * blk:(n + 1) * blk, :], p, k0, n_kb, offs)

    t_refs, r_refs = pipe_refs[:ATT_PIPE], pipe_refs[ATT_PIPE:]

    def step_logits(n, slot):
        p, q0 = where(n)
        t_refs[slot][...] = logits(by_head(qp_ref[p, pl.ds(q0, blk), :]), p, q0 - near, n_kb, True)

    def t_blk_of(slot):
        return lambda hd, b: t_refs[slot][hd * blk:(hd + 1) * blk, b * blk:(b + 1) * blk]

    def step_cumsums(slot):
        r_refs[slot][...] = cumsums(t_blk_of(slot), n_kb)

    def step_values(n, slot):
        p, q0 = where(n)
        acc, offs = weighted_values(t_blk_of(slot),
                                    lambda m: r_refs[slot][m * blk:(m + 1) * blk, :],
                                    p, q0 - near, n_kb, None)
        op_ref[p, pl.ds(q0, blk), :] = acc.astype(_BF16)
        dead_ref[n] = jnp.min(jnp.minimum(offs[0], offs[1]))

    n_steady = n_all - 2
    assert n_steady > 0 and n_steady % ATT_PIPE == 0
    rounds = max(r for r in range(1, ATT_MAX_TRIP_ROUNDS + 1) if (n_steady // ATT_PIPE) % r == 0)
    per_trip = ATT_PIPE * rounds

    def pipe_steps(g, _):
        for u in range(per_trip):
            n = g * per_trip + u
            step_logits(n + 2, (u + 2) % ATT_PIPE)
            step_cumsums((u + 1) % ATT_PIPE)
            step_values(n, u % ATT_PIPE)
        return 0

    step_logits(0, 0)
    step_logits(1, 1)
    step_cumsums(0)
    lax.fori_loop(0, n_steady // per_trip, pipe_steps, 0)
    step_cumsums((n_steady + 1) % ATT_PIPE)
    step_values(n_all - 2, n_steady % ATT_PIPE)
    step_values(n_all - 1, (n_steady + 1) % ATT_PIPE)

    def full_pass(n, _):
        @pl.when(dead_ref[n] <= ATT_DEAD_LOG2)
        def _():
            p, q0 = where(n)
            q_heads = by_head(qp_ref[p, pl.ds(q0, blk), :])
            acc, offs = span(q_heads, p, q0 - near, n_kb, True, None)

            def cond(c):
                return (c[0] >= first * blk) & (c[1] > 0)

            def body(c):
                k0, _, acc, off_a, off_b = c
                pv, (off_a, off_b) = span(q_heads, p, pl.multiple_of(k0, blk), 1, False,
                                          [off_a, off_b])
                alive = jnp.min(jnp.minimum(off_a, off_b)) <= ATT_DEAD_LOG2
                return k0 - blk, alive.astype(jnp.int32), acc + pv, off_a, off_b

            c = lax.while_loop(cond, body,
                               (q0 - near - blk, jnp.int32(1), acc, offs[0], offs[1]))
            op_ref[p, pl.ds(q0, blk), :] = c[2].astype(_BF16)

        return 0

    lax.fori_loop(0, n_all, full_pass, 0)

    for p in range(n_pairs):
        o_ref[:, p * pair:(p + 1) * pair] = op_ref[p, front:front + seq_len, :]


def _attn_call(q, k, v, tri):
    b, seq_len, aw = q.shape
    pair = 2 * HEAD_DIM
    n_kb = ATT_NEAR + 1
    pad_len = n_kb * ATT_BLK - N_META + seq_len
    assert pad_len % ATT_BLK == 0 and aw % pair == 0
    n_all = (aw // pair) * (pad_len // ATT_BLK - ATT_NEAR)
    spec = pl.BlockSpec((None, seq_len, aw), lambda bi: (bi, 0, 0))
    return pl.pallas_call(
        functools.partial(_attn_kernel, seq_len=seq_len),
        grid=(b,),
        in_specs=[spec, spec, spec, _const_spec(tri.shape)],
        out_specs=spec,
        out_shape=jax.ShapeDtypeStruct((b, seq_len, aw), _BF16),
        scratch_shapes=[pltpu.VMEM((aw // pair, pad_len, pair), _BF16)] * 4
        + [pltpu.SMEM((n_all,), _F32)]
        + [pltpu.VMEM((2 * ATT_BLK, n_kb * ATT_BLK), _F32)] * ATT_PIPE
        + [pltpu.VMEM((2 * n_kb * ATT_BLK, 2 * ATT_BLK), _F32)] * ATT_PIPE,
        compiler_params=pltpu.CompilerParams(dimension_semantics=("parallel",),
                                             vmem_limit_bytes=VMEM_LIMIT),
        name="stickbreak_attn",
    )(q, k, v, tri)


def _outmlp_kernel(*refs, n_src, tiles_per_seq, ff_chunk, final_norm):
    src = refs[:n_src]
    ycp_ref, ya_ref, wo_ref, g_ref, wup_ref, wdn_ref = refs[n_src:n_src + 6]
    o_ref = refs[-1]
    n_cp = ycp_ref.shape[1]
    first = lax.rem(pl.program_id(0), tiles_per_seq) == 0
    h1 = _tile_rows(src, 0, o_ref.shape[0], first)
    h1 = h1 + jnp.dot(ycp_ref[...], wo_ref[0:n_cp, :], preferred_element_type=_F32)
    h1 = h1 + jnp.dot(ya_ref[...], wo_ref[n_cp:, :], preferred_element_type=_F32)
    xn = (h1 * _rms_scale(h1) * g_ref[...]).astype(_BF16)
    o_ref[...] = h1
    for c in range(wup_ref.shape[1] // ff_chunk):
        cols = slice(c * ff_chunk, (c + 1) * ff_chunk)
        m = jnp.dot(xn, wup_ref[:, cols], preferred_element_type=_F32)
        act = jnp.square(jnp.maximum(m, 0.0)).astype(_BF16)
        o_ref[...] += jnp.dot(act, wdn_ref[cols, :], preferred_element_type=_F32)
    if final_norm:
        gf_ref = refs[n_src + 6]
        out = o_ref[...]
        o_ref[...] = out * _rms_scale(out) * gf_ref[...]


def _outmlp_call(src, ycp, ya, wo, g, wup, wdn, *, n_tok, seq_len, g_final=None):
    d = wo.shape[1]
    weights = [wo, g, wup, wdn] + ([] if g_final is None else [g_final])
    if g_final is None:
        tm = _seq_tile(seq_len)
        src_arrays, src_specs = _src_specs(src, tm, seq_len)
        row = lambda width: pl.BlockSpec((tm, width), lambda i: (i, 0))
        kern = functools.partial(_outmlp_kernel, n_src=len(src), tiles_per_seq=seq_len // tm,
                                 ff_chunk=FF_CHUNK, final_norm=False)
        return pl.pallas_call(
            kern,
            grid=(n_tok // tm,),
            in_specs=src_specs + [row(ycp.shape[1]), row(ya.shape[1])]
            + [_const_spec(w.shape) for w in weights],
            out_specs=row(d),
            out_shape=jax.ShapeDtypeStruct((n_tok, d), _F32),
            compiler_params=pltpu.CompilerParams(dimension_semantics=("parallel",),
                                                 vmem_limit_bytes=VMEM_LIMIT),
            name="outproj_mlp",
        )(*src_arrays, ycp, ya, *weights)

    (h,) = src
    b, seq = n_tok // seq_len, seq_len - N_META
    tm = X_TILE
    assert seq % tm == 0
    x_rows = lambda width: pl.BlockSpec((pl.Element(tm), pl.Element(width)),
                                        lambda bi, j: (pl.multiple_of(
                                            bi * seq_len + N_META + j * tm, ROW_ALIGN), 0))
    kern = functools.partial(_outmlp_kernel, n_src=1, tiles_per_seq=1, ff_chunk=FF_CHUNK,
                             final_norm=True)
    return pl.pallas_call(
        kern,
        grid=(b, seq // tm),
        in_specs=[x_rows(d), x_rows(ycp.shape[1]), x_rows(ya.shape[1])]
        + [_const_spec(w.shape) for w in weights],
        out_specs=pl.BlockSpec((None, tm, d), lambda bi, j: (bi, j, 0)),
        out_shape=jax.ShapeDtypeStruct((b, seq, d), _F32),
        compiler_params=pltpu.CompilerParams(dimension_semantics=("parallel", "parallel"),
                                             vmem_limit_bytes=VMEM_LIMIT),
        name="outproj_mlp_final",
    )(h, ycp, ya, *weights)


def _block_diag(w_grp):
    g, pg, _ = w_grp.shape
    out = jnp.zeros((g * pg, g * pg), w_grp.dtype)
    for gi in range(g):
        out = out.at[gi * pg:(gi + 1) * pg, gi * pg:(gi + 1) * pg].set(w_grp[gi])
    return out


def kernel(x, meta_tokens, g_mix, w_in, w_conv, w_pool, pool_scale, w_out, g_mlp, w_up, w_down, g_final):
    b, seq, d = x.shape
    depth = w_in.shape[0]
    assert depth >= 2
    cw = w_conv.shape[2]
    pw = pool_scale.shape[1]
    n_a = 3 * cw + pw
    seq_len = N_META + seq
    n_tok = b * seq_len

    j = lax.broadcasted_iota(jnp.int32, (2 * ATT_BLK, 2 * ATT_BLK), 0) % ATT_BLK
    s = lax.broadcasted_iota(jnp.int32, (2 * ATT_BLK, 2 * ATT_BLK), 1)
    tri = ((j >= s) | (s >= ATT_BLK)).astype(_BF16)

    src = (x, meta_tokens.astype(x.dtype))
    for i in range(depth):
        wa = w_in[i, :, :n_a].astype(_BF16)
        wqkv = w_in[i, :, n_a:].astype(_BF16)
        ycp, q, k, v = _inproj_call(
            src, g_mix[i][None], wa, wqkv, w_conv[i], _block_diag(w_pool[i]).astype(_BF16),
            pool_scale[i][None], n_tok=n_tok, seq_len=seq_len)
        aw = q.shape[1]
        to_seq = lambda t: t.reshape(b, seq_len, aw)
        ya = _attn_call(to_seq(q), to_seq(k), to_seq(v), tri).reshape(n_tok, aw)
        h = _outmlp_call(src, ycp, ya, w_out[i].astype(_BF16), g_mlp[i][None],
                         w_up[i].astype(_BF16), w_down[i].astype(_BF16), n_tok=n_tok,
                         seq_len=seq_len, g_final=g_final[None] if i == depth - 1 else None)
        src = (h,)
    return h
```

```python
import functools

import jax
import jax.numpy as jnp
from jax import lax
from jax.experimental import pallas as pl
from jax.experimental.pallas import tpu as pltpu

N_META = 16
HEAD_DIM = 64
CONV_K = 3
POOL_WINDOWS = (2, 4, 8, 16)
EPS = 1e-6

ROW_ALIGN = 16
HALO = 16
TOKEN_TILE_CAP = 704
X_TILE = 512
FF_CHUNK = 512
INPROJ_SPLIT = 2
ATT_BLK = 128
ATT_NEAR = 2
ATT_PIPE = 3
ATT_MAX_TRIP_ROUNDS = 11
ATT_DEAD_LOG2 = 152.0
ATT_MASKED = -1e30
LOG2_E = 1.4426950408889634
VMEM_LIMIT = 56 * 1024 * 1024

_BF16 = jnp.bfloat16
_F32 = jnp.float32


def _seq_tile(seq_len):
    best = None
    for t in range(ROW_ALIGN, TOKEN_TILE_CAP + 1, ROW_ALIGN):
        if seq_len % t == 0:
            best = t
    assert best is not None, seq_len
    return best


def _row_groups(tm, parts):
    units = tm // ROW_ALIGN
    sizes = [(units // parts + (p < units % parts)) * ROW_ALIGN for p in range(parts)]
    starts = [sum(sizes[:p]) for p in range(parts)]
    return [(s, n) for s, n in zip(starts, sizes) if n]


def _rms_scale(x):
    return lax.rsqrt(jnp.mean(x * x, axis=-1, keepdims=True) + EPS)


def _tile_rows(src, r0, n, first):
    if len(src) == 1:
        return src[0][r0:r0 + n, :]
    x_ref, meta_ref = src
    plain = x_ref[r0:r0 + n, :]
    if r0 == 0:
        lead = jnp.concatenate([meta_ref[...], x_ref[0:n - N_META, :]], axis=0)
    else:
        lead = x_ref[r0 - N_META:r0 - N_META + n, :]
    return jnp.where(first, lead, plain)


def _inproj_kernel(*refs, n_src, tm, seq_len, cw, pw, aw):
    src = refs[:n_src]
    (g_ref, wa_ref, wqkv_ref, wconv_ref, wpool_ref, pscale_ref,
     ycp_ref, q_ref, k_ref, v_ref, buf_ref, lvl_ref, cb_ref) = refs[n_src:]
    i = pl.program_id(0)
    first = lax.rem(i, seq_len // tm) == 0
    top = 2 * HALO

    @pl.when(i == 0)
    def _():
        buf_ref[0:top, :] = jnp.zeros((top, cw + pw), _F32)
        lvl_ref[0:top, :] = jnp.zeros((top, pw), _F32)

    base = lax.rem(i * tm, seq_len)
    lane = lax.broadcasted_iota(jnp.int32, (1, pw), 1)
    pg = pw // len(POOL_WINDOWS)
    win = jnp.zeros((1, pw), jnp.int32)
    for gi, w in enumerate(POOL_WINDOWS):
        win = jnp.where((lane >= gi * pg) & (lane < (gi + 1) * pg), w, win)

    def seq_pos(r0, n):
        pos = base + r0 + lax.broadcasted_iota(jnp.int32, (n, 1), 0)
        pos = jnp.where(pos < 0, pos + seq_len, pos)
        return jnp.where(pos >= seq_len, pos - seq_len, pos)

    def project(r0, n):
        x = _tile_rows(src, r0, n, first)
        xn = (x * _rms_scale(x) * g_ref[...]).astype(_BF16)
        ua = jnp.dot(xn, wa_ref[...], preferred_element_type=_F32)
        cb_ref[r0:r0 + n, :] = ua[:, 0:cw]
        buf_ref[top + r0:top + r0 + n, 0:cw] = ua[:, cw:2 * cw] * ua[:, 2 * cw:3 * cw]
        buf_ref[top + r0:top + r0 + n, cw:cw + pw] = ua[:, 3 * cw:3 * cw + pw]
        uq = jnp.dot(xn, wqkv_ref[...], preferred_element_type=_F32)
        q_ref[r0:r0 + n, :] = (uq[:, 0:aw] * (HEAD_DIM ** -0.5 * LOG2_E)).astype(_BF16)
        k_ref[r0:r0 + n, :] = uq[:, aw:2 * aw].astype(_BF16)
        v_ref[r0:r0 + n, :] = uq[:, 2 * aw:3 * aw].astype(_BF16)

    def mix(r0, n):
        tpos = seq_pos(r0, n)
        lo = top + r0
        wc = wconv_ref[...]
        conv = wc[CONV_K - 1:CONV_K, :] * buf_ref[lo:lo + n, 0:cw]
        for s in range(1, CONV_K):
            shifted = buf_ref[lo - s:lo - s + n, 0:cw]
            conv = conv + jnp.where(tpos >= s, shifted, 0.0) * wc[CONV_K - 1 - s:CONV_K - s, :]
        ycp_ref[r0:r0 + n, 0:cw] = (cb_ref[r0:r0 + n, :] * conv).astype(_BF16)

        ext = n + HALO
        lo = HALO + r0
        epos = seq_pos(r0 - HALO, ext)
        cur = buf_ref[lo:lo + ext, cw:cw + pw]
        sums = cur
        shift = 1
        while shift < max(POOL_WINDOWS):
            if shift == 1:
                prev = buf_ref[lo - 1:lo - 1 + ext, cw:cw + pw]
            else:
                lvl_ref[lo:lo + ext, :] = cur
                prev = lvl_ref[lo - shift:lo - shift + ext, :]
            cur = cur + jnp.where(epos >= shift, prev, 0.0)
            shift *= 2
            sums = jnp.where(win >= shift, cur, sums)
        cnt = jnp.minimum(win, tpos + 1).astype(_F32)
        p_in = buf_ref[top + r0:top + r0 + n, cw:cw + pw]
        pooled = (sums[HALO:, :] / cnt - p_in).astype(_BF16)
        y_pool = jnp.dot(pooled, wpool_ref[...], preferred_element_type=_F32) * pscale_ref[...]
        ycp_ref[r0:r0 + n, cw:cw + pw] = y_pool.astype(_BF16)

    groups = _row_groups(tm, INPROJ_SPLIT)
    for r0, n in groups:
        project(r0, n)
    for r0, n in groups:
        mix(r0, n)

    buf_ref[HALO:top, :] = buf_ref[tm + HALO:tm + top, :]


def _const_spec(shape):
    return pl.BlockSpec(shape, lambda *_: (0,) * len(shape), pipeline_mode=pl.Buffered(1))


def _src_specs(src, tm, seq_len):
    if len(src) == 1:
        (h,) = src
        return [h], [pl.BlockSpec((tm, h.shape[1]), lambda i: (i, 0))]
    x, meta = src
    b, seq, d = x.shape
    tiles_per_seq = seq_len // tm
    assert seq % ROW_ALIGN == 0 and N_META % ROW_ALIGN == 0
    x_rows = lambda i: (pl.multiple_of(
        (i // tiles_per_seq) * seq + jnp.maximum(lax.rem(i, tiles_per_seq) * tm - N_META, 0),
        ROW_ALIGN), 0)
    return [x.reshape(b * seq, d), meta], [pl.BlockSpec((pl.Element(tm), pl.Element(d)), x_rows),
                                           _const_spec(meta.shape)]


def _inproj_call(src, g, wa, wqkv, wconv, wpool, pscale, *, n_tok, seq_len):
    cw = wconv.shape[1]
    pw = wpool.shape[0]
    aw = wqkv.shape[1] // 3
    tm = _seq_tile(seq_len)
    src_arrays, src_specs = _src_specs(src, tm, seq_len)
    kern = functools.partial(_inproj_kernel, n_src=len(src), tm=tm, seq_len=seq_len,
                             cw=cw, pw=pw, aw=aw)
    row = lambda width: pl.BlockSpec((tm, width), lambda i: (i, 0))
    return pl.pallas_call(
        kern,
        grid=(n_tok // tm,),
        in_specs=src_specs + [_const_spec(g.shape), _const_spec(wa.shape), _const_spec(wqkv.shape),
                              _const_spec(wconv.shape), _const_spec(wpool.shape),
                              _const_spec(pscale.shape)],
        out_specs=[row(cw + pw), row(aw), row(aw), row(aw)],
        out_shape=[jax.ShapeDtypeStruct((n_tok, cw + pw), _BF16)]
        + [jax.ShapeDtypeStruct((n_tok, aw), _BF16)] * 3,
        scratch_shapes=[pltpu.VMEM((tm + 2 * HALO, cw + pw), _F32),
                        pltpu.VMEM((tm + 2 * HALO, pw), _F32),
                        pltpu.VMEM((tm, cw), _F32)],
        compiler_params=pltpu.CompilerParams(dimension_semantics=("arbitrary",),
                                             vmem_limit_bytes=VMEM_LIMIT),
        name="inproj_mix",
    )(*src_arrays, g, wa, wqkv, wconv, wpool, pscale)


def _softplus2(t):
    return jnp.maximum(t, 0.0) + jnp.log2(1.0 + jnp.exp2(-jnp.abs(t)))


def _hi_lo(x):
    hi = x.astype(_BF16)
    lo = (x - hi.astype(_F32)).astype(_BF16)
    return jnp.concatenate([hi, lo], axis=1)


def _attn_kernel(q_ref, k_ref, v_ref, tri_ref, o_ref, qp_ref, kp_ref, vp_ref, op_ref, dead_ref,
                 *pipe_refs, seq_len):
    blk = ATT_BLK
    pair = 2 * HEAD_DIM
    n_pairs = q_ref.shape[1] // pair
    near = ATT_NEAR * blk
    front = near + blk - N_META
    n_kb = ATT_NEAR + 1
    first = ATT_NEAR
    n_qb = (front + seq_len) // blk - first
    n_all = n_pairs * n_qb
    lane = lax.broadcasted_iota(jnp.int32, (1, pair), 1)
    head_lanes = [lane < HEAD_DIM, lane >= HEAD_DIM]
    causal = (lax.broadcasted_iota(jnp.int32, (blk, blk), 1)
              < lax.broadcasted_iota(jnp.int32, (blk, blk), 0))
    nt = (((1,), (1,)), ((), ()))

    for src, dst in ((q_ref, qp_ref), (k_ref, kp_ref), (v_ref, vp_ref)):
        for p in range(n_pairs):
            dst[p, 0:front, :] = jnp.zeros((front, pair), _BF16)
            dst[p, front:front + seq_len, :] = src[:, p * pair:(p + 1) * pair]

    def by_head(x):
        return jnp.concatenate([jnp.where(m, x, jnp.zeros_like(x)) for m in head_lanes], axis=0)

    def where(n):
        p = n // n_qb
        return p, pl.multiple_of((first + n - p * n_qb) * blk, blk)

    def logits(q_heads, p, k0, n_kb, diag):
        kb = kp_ref[p, pl.ds(k0, n_kb * blk), :]
        t = lax.dot_general(q_heads, kb, nt, preferred_element_type=_F32)
        if diag:
            last = jnp.where(jnp.concatenate([causal, causal], axis=0),
                             t[:, (n_kb - 1) * blk:], ATT_MASKED)
            t = last if n_kb == 1 else jnp.concatenate([t[:, :(n_kb - 1) * blk], last], axis=1)
        return t

    def cumsums(t_blk, n_kb):
        parts = [_hi_lo(_softplus2(t_blk(hd, b))) for hd in range(2) for b in range(n_kb)]
        return jnp.dot(jnp.concatenate(parts, axis=0), tri_ref[...], preferred_element_type=_F32)

    def weighted_values(t_blk, r_blk, p, k0, n_kb, offs):
        ws, new_offs = [], []
        for hd in range(2):
            off = None if offs is None else offs[hd]
            w_hd = [None] * n_kb
            for b in range(n_kb - 1, -1, -1):
                rb = r_blk(hd * n_kb + b)
                arg = t_blk(hd, b) - rb
                if off is not None:
                    arg = arg - off
                w_hd[b] = jnp.exp2(arg)
                tot = jnp.broadcast_to(rb[:, 0:1], (blk, blk))
                off = tot if off is None else off + tot
            ws += w_hd
            new_offs.append(off)
        a = jnp.concatenate(ws, axis=1).astype(_BF16)
        vb = vp_ref[p, pl.ds(k0, n_kb * blk), :]
        return jnp.dot(a, by_head(vb), preferred_element_type=_F32), new_offs

    def span(q_heads, p, k0, n_kb, diag, offs):
        t = logits(q_heads, p, k0, n_kb, diag)
        t_blk = lambda hd, b: t[hd * blk:(hd + 1) * blk, b * blk:(b + 1) * blk]
        r = cumsums(t_blk, n_kb)
        return weighted_values(t_blk, lambda n: r[n * blk:(n + 1) * blk, :], p, k0, n_kb, offs)

    t_refs, r_refs = pipe_refs[:ATT_PIPE], pipe_refs[ATT_PIPE:]

    def step_logits(n, slot):
        p, q0 = where(n)
        t_refs[slot][...] = logits(by_head(qp_ref[p, pl.ds(q0, blk), :]), p, q0 - near, n_kb, True)

    def t_blk_of(slot):
        return lambda hd, b: t_refs[slot][hd * blk:(hd + 1) * blk, b * blk:(b + 1) * blk]

    def step_cumsums(slot):
        r_refs[slot][...] = cumsums(t_blk_of(slot), n_kb)

    def step_values(n, slot):
        p, q0 = where(n)
        acc, offs = weighted_values(t_blk_of(slot),
                                    lambda m: r_refs[slot][m * blk:(m + 1) * blk, :],
                                    p, q0 - near, n_kb, None)
        op_ref[p, pl.ds(q0, blk), :] = acc.astype(_BF16)
        dead_ref[n] = jnp.min(jnp.minimum(offs[0], offs[1]))

    n_steady = n_all - 2
    assert n_steady > 0 and n_steady % ATT_PIPE == 0
    rounds = max(r for r in range(1, ATT_MAX_TRIP_ROUNDS + 1) if (n_steady // ATT_PIPE) % r == 0)
    per_trip = ATT_PIPE * rounds

    def pipe_steps(g, _):
        for u in range(per_trip):
            n = g * per_trip + u
            step_logits(n + 2, (u + 2) % ATT_PIPE)
            step_cumsums((u + 1) % ATT_PIPE)
            step_values(n, u % ATT_PIPE)
        return 0

    step_logits(0, 0)
    step_logits(1, 1)
    step_cumsums(0)
    lax.fori_loop(0, n_steady // per_trip, pipe_steps, 0)
    step_cumsums((n_steady + 1) % ATT_PIPE)
    step_values(n_all - 2, n_steady % ATT_PIPE)
    step_values(n_all - 1, (n_steady + 1) % ATT_PIPE)

    def full_pass(n, _):
        @pl.when(dead_ref[n] <= ATT_DEAD_LOG2)
        def _():
            p, q0 = where(n)
            q_heads = by_head(qp_ref[p, pl.ds(q0, blk), :])
            acc, offs = span(q_heads, p, q0 - near, n_kb, True, None)

            def cond(c):
                return (c[0] >= first * blk) & (c[1] > 0)

            def body(c):
                k0, _, acc, off_a, off_b = c
                pv, (off_a, off_b) = span(q_heads, p, pl.multiple_of(k0, blk), 1, False,
                                          [off_a, off_b])
                alive = jnp.min(jnp.minimum(off_a, off_b)) <= ATT_DEAD_LOG2
                return k0 - blk, alive.astype(jnp.int32), acc + pv, off_a, off_b

            c = lax.while_loop(cond, body,
                               (q0 - near - blk, jnp.int32(1), acc, offs[0], offs[1]))
            op_ref[p, pl.ds(q0, blk), :] = c[2].astype(_BF16)

        return 0

    lax.fori_loop(0, n_all, full_pass, 0)

    for p in range(n_pairs):
        o_ref[:, p * pair:(p + 1) * pair] = op_ref[p, front:front + seq_len, :]


def _attn_call(q, k, v, tri):
    b, seq_len, aw = q.shape
    pair = 2 * HEAD_DIM
    n_kb = ATT_NEAR + 1
    pad_len = n_kb * ATT_BLK - N_META + seq_len
    assert pad_len % ATT_BLK == 0 and aw % pair == 0
    n_all = (aw // pair) * (pad_len // ATT_BLK - ATT_NEAR)
    spec = pl.BlockSpec((None, seq_len, aw), lambda bi: (bi, 0, 0))
    return pl.pallas_call(
        functools.partial(_attn_kernel, seq_len=seq_len),
        grid=(b,),
        in_specs=[spec, spec, spec, _const_spec(tri.shape)],
        out_specs=spec,
        out_shape=jax.ShapeDtypeStruct((b, seq_len, aw), _BF16),
        scratch_shapes=[pltpu.VMEM((aw // pair, pad_len, pair), _BF16)] * 4
        + [pltpu.SMEM((n_all,), _F32)]
        + [pltpu.VMEM((2 * ATT_BLK, n_kb * ATT_BLK), _F32)] * ATT_PIPE
        + [pltpu.VMEM((2 * n_kb * ATT_BLK, ATT_BLK), _F32)] * ATT_PIPE,
        compiler_params=pltpu.CompilerParams(dimension_semantics=("parallel",),
                                             vmem_limit_bytes=VMEM_LIMIT),
        name="stickbreak_attn",
    )(q, k, v, tri)


def _outmlp_kernel(*refs, n_src, tiles_per_seq, ff_chunk, final_norm):
    src = refs[:n_src]
    ycp_ref, ya_ref, wo_ref, g_ref, wup_ref, wdn_ref = refs[n_src:n_src + 6]
    o_ref = refs[-1]
    n_cp = ycp_ref.shape[1]
    first = lax.rem(pl.program_id(0), tiles_per_seq) == 0
    h1 = _tile_rows(src, 0, o_ref.shape[0], first)
    h1 = h1 + jnp.dot(ycp_ref[...], wo_ref[0:n_cp, :], preferred_element_type=_F32)
    h1 = h1 + jnp.dot(ya_ref[...], wo_ref[n_cp:, :], preferred_element_type=_F32)
    xn = (h1 * _rms_scale(h1) * g_ref[...]).astype(_BF16)
    o_ref[...] = h1
    for c in range(wup_ref.shape[1] // ff_chunk):
        cols = slice(c * ff_chunk, (c + 1) * ff_chunk)
        m = jnp.dot(xn, wup_ref[:, cols], preferred_element_type=_F32)
        act = jnp.square(jnp.maximum(m, 0.0)).astype(_BF16)
        o_ref[...] += jnp.dot(act, wdn_ref[cols, :], preferred_element_type=_F32)
    if final_norm:
        gf_ref = refs[n_src + 6]
        out = o_ref[...]
        o_ref[...] = out * _rms_scale(out) * gf_ref[...]


def _outmlp_call(src, ycp, ya, wo, g, wup, wdn, *, n_tok, seq_len, g_final=None):
    d = wo.shape[1]
    weights = [wo, g, wup, wdn] + ([] if g_final is None else [g_final])
    if g_final is None:
        tm = _seq_tile(seq_len)
        src_arrays, src_specs = _src_specs(src, tm, seq_len)
        row = lambda width: pl.BlockSpec((tm, width), lambda i: (i, 0))
        kern = functools.partial(_outmlp_kernel, n_src=len(src), tiles_per_seq=seq_len // tm,
                                 ff_chunk=FF_CHUNK, final_norm=False)
        return pl.pallas_call(
            kern,
            grid=(n_tok // tm,),
            in_specs=src_specs + [row(ycp.shape[1]), row(ya.shape[1])]
            + [_const_spec(w.shape) for w in weights],
            out_specs=row(d),
            out_shape=jax.ShapeDtypeStruct((n_tok, d), _F32),
            compiler_params=pltpu.CompilerParams(dimension_semantics=("parallel",),
                                                 vmem_limit_bytes=VMEM_LIMIT),
            name="outproj_mlp",
        )(*src_arrays, ycp, ya, *weights)

    (h,) = src
    b, seq = n_tok // seq_len, seq_len - N_META
    tm = X_TILE
    assert seq % tm == 0
    x_rows = lambda width: pl.BlockSpec((pl.Element(tm), pl.Element(width)),
                                        lambda bi, j: (pl.multiple_of(
                                            bi * seq_len + N_META + j * tm, ROW_ALIGN), 0))
    kern = functools.partial(_outmlp_kernel, n_src=1, tiles_per_seq=1, ff_chunk=FF_CHUNK,
                             final_norm=True)
    return pl.pallas_call(
        kern,
        grid=(b, seq // tm),
        in_specs=[x_rows(d), x_rows(ycp.shape[1]), x_rows(ya.shape[1])]
        + [_const_spec(w.shape) for w in weights],
        out_specs=pl.BlockSpec((None, tm, d), lambda bi, j: (bi, j, 0)),
        out_shape=jax.ShapeDtypeStruct((b, seq, d), _F32),
        compiler_params=pltpu.CompilerParams(dimension_semantics=("parallel", "parallel"),
                                             vmem_limit_bytes=VMEM_LIMIT),
        name="outproj_mlp_final",
    )(h, ycp, ya, *weights)


def _block_diag(w_grp):
    g, pg, _ = w_grp.shape
    out = jnp.zeros((g * pg, g * pg), w_grp.dtype)
    for gi in range(g):
        out = out.at[gi * pg:(gi + 1) * pg, gi * pg:(gi + 1) * pg].set(w_grp[gi])
    return out


def kernel(x, meta_tokens, g_mix, w_in, w_conv, w_pool, pool_scale, w_out, g_mlp, w_up, w_down, g_final):
    b, seq, d = x.shape
    depth = w_in.shape[0]
    assert depth >= 2
    cw = w_conv.shape[2]
    pw = pool_scale.shape[1]
    n_a = 3 * cw + pw
    seq_len = N_META + seq
    n_tok = b * seq_len

    j = lax.broadcasted_iota(jnp.int32, (2 * ATT_BLK, ATT_BLK), 0) % ATT_BLK
    s = lax.broadcasted_iota(jnp.int32, (2 * ATT_BLK, ATT_BLK), 1)
    tri = (j >= s).astype(_BF16)

    src = (x, meta_tokens.astype(x.dtype))
    for i in range(depth):
        wa = w_in[i, :, :n_a].astype(_BF16)
        wqkv = w_in[i, :, n_a:].astype(_BF16)
        ycp, q, k, v = _inproj_call(
            src, g_mix[i][None], wa, wqkv, w_conv[i], _block_diag(w_pool[i]).astype(_BF16),
            pool_scale[i][None], n_tok=n_tok, seq_len=seq_len)
        aw = q.shape[1]
        to_seq = lambda t: t.reshape(b, seq_len, aw)
        ya = _attn_call(to_seq(q), to_seq(k), to_seq(v), tri).reshape(n_tok, aw)
        h = _outmlp_call(src, ycp, ya, w_out[i].astype(_BF16), g_mlp[i][None],
                         w_up[i].astype(_BF16), w_down[i].astype(_BF16), n_tok=n_tok,
                         seq_len=seq_len, g_final=g_final[None] if i == depth - 1 else None)
        src = (h,)
    return h
```

```python
import functools

import jax
import jax.numpy as jnp
from jax import lax
from jax.experimental import pallas as pl
from jax.experimental.pallas import tpu as pltpu

N_META = 16
HEAD_DIM = 64
CONV_K = 3
POOL_WINDOWS = (2, 4, 8, 16)
EPS = 1e-6

ROW_ALIGN = 16
HALO = 16
TOKEN_TILE_CAP = 704
X_TILE = 512
FF_CHUNK = 512
INPROJ_SPLIT = 2
ATT_BLK = 128
ATT_NEAR = 2
ATT_PIPE = 3
ATT_MAX_TRIP_ROUNDS = 11
ATT_DEAD_LOG2 = 152.0
ATT_MASKED = -1e30
LOG2_E = 1.4426950408889634
VMEM_LIMIT = 56 * 1024 * 1024

_BF16 = jnp.bfloat16
_F32 = jnp.float32


def _seq_tile(seq_len):
    best = None
    for t in range(ROW_ALIGN, TOKEN_TILE_CAP + 1, ROW_ALIGN):
        if seq_len % t == 0:
            best = t
    assert best is not None, seq_len
    return best


def _row_groups(tm, parts):
    units = tm // ROW_ALIGN
    sizes = [(units // parts + (p < units % parts)) * ROW_ALIGN for p in range(parts)]
    starts = [sum(sizes[:p]) for p in range(parts)]
    return [(s, n) for s, n in zip(starts, sizes) if n]


def _rms_scale(x):
    return lax.rsqrt(jnp.mean(x * x, axis=-1, keepdims=True) + EPS)


def _tile_rows(src, r0, n, first):
    if len(src) == 1:
        return src[0][r0:r0 + n, :]
    x_ref, meta_ref = src
    plain = x_ref[r0:r0 + n, :]
    if r0 == 0:
        lead = jnp.concatenate([meta_ref[...], x_ref[0:n - N_META, :]], axis=0)
    else:
        lead = x_ref[r0 - N_META:r0 - N_META + n, :]
    return jnp.where(first, lead, plain)


def _inproj_kernel(*refs, n_src, tm, seq_len, cw, pw, aw):
    src = refs[:n_src]
    (g_ref, wa_ref, wqkv_ref, wconv_ref, wpool_ref, pscale_ref,
     ycp_ref, q_ref, k_ref, v_ref, buf_ref, lvl_ref, cb_ref) = refs[n_src:]
    i = pl.program_id(0)
    first = lax.rem(i, seq_len // tm) == 0
    top = 2 * HALO

    @pl.when(i == 0)
    def _():
        buf_ref[0:top, :] = jnp.zeros((top, cw + pw), _F32)
        lvl_ref[0:top, :] = jnp.zeros((top, pw), _F32)

    base = lax.rem(i * tm, seq_len)
    lane = lax.broadcasted_iota(jnp.int32, (1, pw), 1)
    pg = pw // len(POOL_WINDOWS)
    win = jnp.zeros((1, pw), jnp.int32)
    for gi, w in enumerate(POOL_WINDOWS):
        win = jnp.where((lane >= gi * pg) & (lane < (gi + 1) * pg), w, win)

    def seq_pos(r0, n):
        pos = base + r0 + lax.broadcasted_iota(jnp.int32, (n, 1), 0)
        pos = jnp.where(pos < 0, pos + seq_len, pos)
        return jnp.where(pos >= seq_len, pos - seq_len, pos)

    def project(r0, n):
        x = _tile_rows(src, r0, n, first)
        xn = (x * _rms_scale(x) * g_ref[...]).astype(_BF16)
        ua = jnp.dot(xn, wa_ref[...], preferred_element_type=_F32)
        cb_ref[r0:r0 + n, :] = ua[:, 0:cw]
        buf_ref[top + r0:top + r0 + n, 0:cw] = ua[:, cw:2 * cw] * ua[:, 2 * cw:3 * cw]
        buf_ref[top + r0:top + r0 + n, cw:cw + pw] = ua[:, 3 * cw:3 * cw + pw]
        uq = jnp.dot(xn, wqkv_ref[...], preferred_element_type=_F32)
        q_ref[r0:r0 + n, :] = (uq[:, 0:aw] * (HEAD_DIM ** -0.5 * LOG2_E)).astype(_BF16)
        k_ref[r0:r0 + n, :] = uq[:, aw:2 * aw].astype(_BF16)
        v_ref[r0:r0 + n, :] = uq[:, 2 * aw:3 * aw].astype(_BF16)

    def mix(r0, n):
        tpos = seq_pos(r0, n)
        lo = top + r0
        wc = wconv_ref[...]
        conv = wc[CONV_K - 1:CONV_K, :] * buf_ref[lo:lo + n, 0:cw]
        for s in range(1, CONV_K):
            shifted = buf_ref[lo - s:lo - s + n, 0:cw]
            conv = conv + jnp.where(tpos >= s, shifted, 0.0) * wc[CONV_K - 1 - s:CONV_K - s, :]
        ycp_ref[r0:r0 + n, 0:cw] = (cb_ref[r0:r0 + n, :] * conv).astype(_BF16)

        ext = n + HALO
        lo = HALO + r0
        epos = seq_pos(r0 - HALO, ext)
        cur = buf_ref[lo:lo + ext, cw:cw + pw]
        sums = cur
        shift = 1
        while shift < max(POOL_WINDOWS):
            if shift == 1:
                prev = buf_ref[lo - 1:lo - 1 + ext, cw:cw + pw]
            else:
                lvl_ref[lo:lo + ext, :] = cur
                prev = lvl_ref[lo - shift:lo - shift + ext, :]
            cur = cur + jnp.where(epos >= shift, prev, 0.0)
            shift *= 2
            sums = jnp.where(win >= shift, cur, sums)
        cnt = jnp.minimum(win, tpos + 1).astype(_F32)
        p_in = buf_ref[top + r0:top + r0 + n, cw:cw + pw]
        pooled = (sums[HALO:, :] / cnt - p_in).astype(_BF16)
        y_pool = jnp.dot(pooled, wpool_ref[...], preferred_element_type=_F32) * pscale_ref[...]
        ycp_ref[r0:r0 + n, cw:cw + pw] = y_pool.astype(_BF16)

    groups = _row_groups(tm, INPROJ_SPLIT)
    for r0, n in groups:
        project(r0, n)
    for r0, n in groups:
        mix(r0, n)

    buf_ref[HALO:top, :] = buf_ref[tm + HALO:tm + top, :]


def _const_spec(shape):
    return pl.BlockSpec(shape, lambda *_: (0,) * len(shape), pipeline_mode=pl.Buffered(1))


def _src_specs(src, tm, seq_len):
    if len(src) == 1:
        (h,) = src
        return [h], [pl.BlockSpec((tm, h.shape[1]), lambda i: (i, 0))]
    x, meta = src
    b, seq, d = x.shape
    tiles_per_seq = seq_len // tm
    assert seq % ROW_ALIGN == 0 and N_META % ROW_ALIGN == 0
    x_rows = lambda i: (pl.multiple_of(
        (i // tiles_per_seq) * seq + jnp.maximum(lax.rem(i, tiles_per_seq) * tm - N_META, 0),
        ROW_ALIGN), 0)
    return [x.reshape(b * seq, d), meta], [pl.BlockSpec((pl.Element(tm), pl.Element(d)), x_rows),
                                           _const_spec(meta.shape)]


def _inproj_call(src, g, wa, wqkv, wconv, wpool, pscale, *, n_tok, seq_len):
    cw = wconv.shape[1]
    pw = wpool.shape[0]
    aw = wqkv.shape[1] // 3
    tm = _seq_tile(seq_len)
    src_arrays, src_specs = _src_specs(src, tm, seq_len)
    kern = functools.partial(_inproj_kernel, n_src=len(src), tm=tm, seq_len=seq_len,
                             cw=cw, pw=pw, aw=aw)
    row = lambda width: pl.BlockSpec((tm, width), lambda i: (i, 0))
    return pl.pallas_call(
        kern,
        grid=(n_tok // tm,),
        in_specs=src_specs + [_const_spec(g.shape), _const_spec(wa.shape), _const_spec(wqkv.shape),
                              _const_spec(wconv.shape), _const_spec(wpool.shape),
                              _const_spec(pscale.shape)],
        out_specs=[row(cw + pw), row(aw), row(aw), row(aw)],
        out_shape=[jax.ShapeDtypeStruct((n_tok, cw + pw), _BF16)]
        + [jax.ShapeDtypeStruct((n_tok, aw), _BF16)] * 3,
        scratch_shapes=[pltpu.VMEM((tm + 2 * HALO, cw + pw), _F32),
                        pltpu.VMEM((tm + 2 * HALO, pw), _F32),
                        pltpu.VMEM((tm, cw), _F32)],
        compiler_params=pltpu.CompilerParams(dimension_semantics=("arbitrary",),
                                             vmem_limit_bytes=VMEM_LIMIT),
        name="inproj_mix",
    )(*src_arrays, g, wa, wqkv, wconv, wpool, pscale)


def _softplus2(t):
    return jnp.maximum(t, 0.0) + jnp.log2(1.0 + jnp.exp2(-jnp.abs(t)))


def _hi_lo(x):
    hi = x.astype(_BF16)
    lo = (x - hi.astype(_F32)).astype(_BF16)
    return jnp.concatenate([hi, lo], axis=1)


def _attn_kernel(q_ref, k_ref, v_ref, tri_ref, o_ref, qp_ref, kp_ref, vp_ref, op_ref, kt_ref,
                 dead_ref, *pipe_refs, seq_len):
    blk = ATT_BLK
    pair = 2 * HEAD_DIM
    n_pairs = q_ref.shape[1] // pair
    near = ATT_NEAR * blk
    front = near + blk - N_META
    n_kb = ATT_NEAR + 1
    first = ATT_NEAR
    n_qb = (front + seq_len) // blk - first
    n_all = n_pairs * n_qb
    lane = lax.broadcasted_iota(jnp.int32, (1, pair), 1)
    head_lanes = [lane < HEAD_DIM, lane >= HEAD_DIM]
    causal = (lax.broadcasted_iota(jnp.int32, (blk, blk), 1)
              < lax.broadcasted_iota(jnp.int32, (blk, blk), 0))
    nt = (((1,), (1,)), ((), ()))

    for src, dst in ((q_ref, qp_ref), (k_ref, kp_ref), (v_ref, vp_ref)):
        for p in range(n_pairs):
            dst[p, 0:front, :] = jnp.zeros((front, pair), _BF16)
            dst[p, front:front + seq_len, :] = src[:, p * pair:(p + 1) * pair]
    for p in range(n_pairs):
        for j in range((front + seq_len) // blk):
            kt_ref[p, j] = kp_ref[p, j * blk:(j + 1) * blk, :].T

    def by_head(x):
        return jnp.concatenate([jnp.where(m, x, jnp.zeros_like(x)) for m in head_lanes], axis=0)

    def where(n):
        p = n // n_qb
        return p, pl.multiple_of((first + n - p * n_qb) * blk, blk)

    def logits(q_heads, p, k0, n_kb, diag):
        kb_t = jnp.concatenate([kt_ref[p, k0 // blk + b] for b in range(n_kb)], axis=1)
        t = jnp.dot(q_heads, kb_t, preferred_element_type=_F32)
        if diag:
            last = jnp.where(jnp.concatenate([causal, causal], axis=0),
                             t[:, (n_kb - 1) * blk:], ATT_MASKED)
            t = last if n_kb == 1 else jnp.concatenate([t[:, :(n_kb - 1) * blk], last], axis=1)
        return t

    def cumsums(t_blk, n_kb):
        parts = [_hi_lo(_softplus2(t_blk(hd, b))) for hd in range(2) for b in range(n_kb)]
        return jnp.dot(jnp.concatenate(parts, axis=0), tri_ref[...], preferred_element_type=_F32)

    def weighted_values(t_blk, r_blk, p, k0, n_kb, offs):
        ws, new_offs = [], []
        for hd in range(2):
            off = None if offs is None else offs[hd]
            w_hd = [None] * n_kb
            for b in range(n_kb - 1, -1, -1):
                rb = r_blk(hd * n_kb + b)
                arg = t_blk(hd, b) - rb
                if off is not None:
                    arg = arg - off
                w_hd[b] = jnp.exp2(arg)
                tot = jnp.broadcast_to(rb[:, 0:1], (blk, blk))
                off = tot if off is None else off + tot
            ws += w_hd
            new_offs.append(off)
        a = jnp.concatenate(ws, axis=1).astype(_BF16)
        vb = vp_ref[p, pl.ds(k0, n_kb * blk), :]
        return jnp.dot(a, by_head(vb), preferred_element_type=_F32), new_offs

    def span(q_heads, p, k0, n_kb, diag, offs):
        t = logits(q_heads, p, k0, n_kb, diag)
        t_blk = lambda hd, b: t[hd * blk:(hd + 1) * blk, b * blk:(b + 1) * blk]
        r = cumsums(t_blk, n_kb)
        return weighted_values(t_blk, lambda n: r[n * blk:(n + 1) * blk, :], p, k0, n_kb, offs)

    t_refs, r_refs = pipe_refs[:ATT_PIPE], pipe_refs[ATT_PIPE:]

    def step_logits(n, slot):
        p, q0 = where(n)
        t_refs[slot][...] = logits(by_head(qp_ref[p, pl.ds(q0, blk), :]), p, q0 - near, n_kb, True)

    def t_blk_of(slot):
        return lambda hd, b: t_refs[slot][hd * blk:(hd + 1) * blk, b * blk:(b + 1) * blk]

    def step_cumsums(slot):
        r_refs[slot][...] = cumsums(t_blk_of(slot), n_kb)

    def step_values(n, slot):
        p, q0 = where(n)
        acc, offs = weighted_values(t_blk_of(slot),
                                    lambda m: r_refs[slot][m * blk:(m + 1) * blk, :],
                                    p, q0 - near, n_kb, None)
        op_ref[p, pl.ds(q0, blk), :] = acc.astype(_BF16)
        dead_ref[n] = jnp.min(jnp.minimum(offs[0], offs[1]))

    n_steady = n_all - 2
    assert n_steady > 0 and n_steady % ATT_PIPE == 0
    rounds = max(r for r in range(1, ATT_MAX_TRIP_ROUNDS + 1) if (n_steady // ATT_PIPE) % r == 0)
    per_trip = ATT_PIPE * rounds

    def pipe_steps(g, _):
        for u in range(per_trip):
            n = g * per_trip + u
            step_logits(n + 2, (u + 2) % ATT_PIPE)
            step_cumsums((u + 1) % ATT_PIPE)
            step_values(n, u % ATT_PIPE)
        return 0

    step_logits(0, 0)
    step_logits(1, 1)
    step_cumsums(0)
    lax.fori_loop(0, n_steady // per_trip, pipe_steps, 0)
    step_cumsums((n_steady + 1) % ATT_PIPE)
    step_values(n_all - 2, n_steady % ATT_PIPE)
    step_values(n_all - 1, (n_steady + 1) % ATT_PIPE)

    def full_pass(n, _):
        @pl.when(dead_ref[n] <= ATT_DEAD_LOG2)
        def _():
            p, q0 = where(n)
            q_heads = by_head(qp_ref[p, pl.ds(q0, blk), :])
            acc, offs = span(q_heads, p, q0 - near, n_kb, True, None)

            def cond(c):
                return (c[0] >= first * blk) & (c[1] > 0)

            def body(c):
                k0, _, acc, off_a, off_b = c
                pv, (off_a, off_b) = span(q_heads, p, pl.multiple_of(k0, blk), 1, False,
                                          [off_a, off_b])
                alive = jnp.min(jnp.minimum(off_a, off_b)) <= ATT_DEAD_LOG2
                return k0 - blk, alive.astype(jnp.int32), acc + pv, off_a, off_b

            c = lax.while_loop(cond, body,
                               (q0 - near - blk, jnp.int32(1), acc, offs[0], offs[1]))
            op_ref[p, pl.ds(q0, blk), :] = c[2].astype(_BF16)

        return 0

    lax.fori_loop(0, n_all, full_pass, 0)

    for p in range(n_pairs):
        o_ref[:, p * pair:(p + 1) * pair] = op_ref[p, front:front + seq_len, :]


def _attn_call(q, k, v, tri):
    b, seq_len, aw = q.shape
    pair = 2 * HEAD_DIM
    n_kb = ATT_NEAR + 1
    pad_len = n_kb * ATT_BLK - N_META + seq_len
    assert pad_len % ATT_BLK == 0 and aw % pair == 0
    n_all = (aw // pair) * (pad_len // ATT_BLK - ATT_NEAR)
    spec = pl.BlockSpec((None, seq_len, aw), lambda bi: (bi, 0, 0))
    return pl.pallas_call(
        functools.partial(_attn_kernel, seq_len=seq_len),
        grid=(b,),
        in_specs=[spec, spec, spec, _const_spec(tri.shape)],
        out_specs=spec,
        out_shape=jax.ShapeDtypeStruct((b, seq_len, aw), _BF16),
        scratch_shapes=[pltpu.VMEM((aw // pair, pad_len, pair), _BF16)] * 4
        + [pltpu.VMEM((aw // pair, pad_len // ATT_BLK, pair, ATT_BLK), _BF16),
           pltpu.SMEM((n_all,), _F32)]
        + [pltpu.VMEM((2 * ATT_BLK, n_kb * ATT_BLK), _F32)] * ATT_PIPE
        + [pltpu.VMEM((2 * n_kb * ATT_BLK, ATT_BLK), _F32)] * ATT_PIPE,
        compiler_params=pltpu.CompilerParams(dimension_semantics=("parallel",),
                                             vmem_limit_bytes=VMEM_LIMIT),
        name="stickbreak_attn",
    )(q, k, v, tri)


def _outmlp_kernel(*refs, n_src, tiles_per_seq, ff_chunk, final_norm):
    src = refs[:n_src]
    ycp_ref, ya_ref, wo_ref, g_ref, wup_ref, wdn_ref = refs[n_src:n_src + 6]
    o_ref = refs[-1]
    n_cp = ycp_ref.shape[1]
    first = lax.rem(pl.program_id(0), tiles_per_seq) == 0
    h1 = _tile_rows(src, 0, o_ref.shape[0], first)
    h1 = h1 + jnp.dot(ycp_ref[...], wo_ref[0:n_cp, :], preferred_element_type=_F32)
    h1 = h1 + jnp.dot(ya_ref[...], wo_ref[n_cp:, :], preferred_element_type=_F32)
    xn = (h1 * _rms_scale(h1) * g_ref[...]).astype(_BF16)
    o_ref[...] = h1
    for c in range(wup_ref.shape[1] // ff_chunk):
        cols = slice(c * ff_chunk, (c + 1) * ff_chunk)
        m = jnp.dot(xn, wup_ref[:, cols], preferred_element_type=_F32)
        act = jnp.square(jnp.maximum(m, 0.0)).astype(_BF16)
        o_ref[...] += jnp.dot(act, wdn_ref[cols, :], preferred_element_type=_F32)
    if final_norm:
        gf_ref = refs[n_src + 6]
        out = o_ref[...]
        o_ref[...] = out * _rms_scale(out) * gf_ref[...]


def _outmlp_call(src, ycp, ya, wo, g, wup, wdn, *, n_tok, seq_len, g_final=None):
    d = wo.shape[1]
    weights = [wo, g, wup, wdn] + ([] if g_final is None else [g_final])
    if g_final is None:
        tm = _seq_tile(seq_len)
        src_arrays, src_specs = _src_specs(src, tm, seq_len)
        row = lambda width: pl.BlockSpec((tm, width), lambda i: (i, 0))
        kern = functools.partial(_outmlp_kernel, n_src=len(src), tiles_per_seq=seq_len // tm,
                                 ff_chunk=FF_CHUNK, final_norm=False)
        return pl.pallas_call(
            kern,
            grid=(n_tok // tm,),
            in_specs=src_specs + [row(ycp.shape[1]), row(ya.shape[1])]
            + [_const_spec(w.shape) for w in weights],
            out_specs=row(d),
            out_shape=jax.ShapeDtypeStruct((n_tok, d), _F32),
            compiler_params=pltpu.CompilerParams(dimension_semantics=("parallel",),
                                                 vmem_limit_bytes=VMEM_LIMIT),
            name="outproj_mlp",
        )(*src_arrays, ycp, ya, *weights)

    (h,) = src
    b, seq = n_tok // seq_len, seq_len - N_META
    tm = X_TILE
    assert seq % tm == 0
    x_rows = lambda width: pl.BlockSpec((pl.Element(tm), pl.Element(width)),
                                        lambda bi, j: (pl.multiple_of(
                                            bi * seq_len + N_META + j * tm, ROW_ALIGN), 0))
    kern = functools.partial(_outmlp_kernel, n_src=1, tiles_per_seq=1, ff_chunk=FF_CHUNK,
                             final_norm=True)
    return pl.pallas_call(
        kern,
        grid=(b, seq // tm),
        in_specs=[x_rows(d), x_rows(ycp.shape[1]), x_rows(ya.shape[1])]
        + [_const_spec(w.shape) for w in weights],
        out_specs=pl.BlockSpec((None, tm, d), lambda bi, j: (bi, j, 0)),
        out_shape=jax.ShapeDtypeStruct((b, seq, d), _F32),
        compiler_params=pltpu.CompilerParams(dimension_semantics=("parallel", "parallel"),
                                             vmem_limit_bytes=VMEM_LIMIT),
        name="outproj_mlp_final",
    )(h, ycp, ya, *weights)


def _block_diag(w_grp):
    g, pg, _ = w_grp.shape
    out = jnp.zeros((g * pg, g * pg), w_grp.dtype)
    for gi in range(g):
        out = out.at[gi * pg:(gi + 1) * pg, gi * pg:(gi + 1) * pg].set(w_grp[gi])
    return out


def kernel(x, meta_tokens, g_mix, w_in, w_conv, w_pool, pool_scale, w_out, g_mlp, w_up, w_down, g_final):
    b, seq, d = x.shape
    depth = w_in.shape[0]
    assert depth >= 2
    cw = w_conv.shape[2]
    pw = pool_scale.shape[1]
    n_a = 3 * cw + pw
    seq_len = N_META + seq
    n_tok = b * seq_len

    j = lax.broadcasted_iota(jnp.int32, (2 * ATT_BLK, ATT_BLK), 0) % ATT_BLK
    s = lax.broadcasted_iota(jnp.int32, (2 * ATT_BLK, ATT_BLK), 1)
    tri = (j >= s).astype(_BF16)

    src = (x, meta_tokens.astype(x.dtype))
    for i in range(depth):
        wa = w_in[i, :, :n_a].astype(_BF16)
        wqkv = w_in[i, :, n_a:].astype(_BF16)
        ycp, q, k, v = _inproj_call(
            src, g_mix[i][None], wa, wqkv, w_conv[i], _block_diag(w_pool[i]).astype(_BF16),
            pool_scale[i][None], n_tok=n_tok, seq_len=seq_len)
        aw = q.shape[1]
        to_seq = lambda t: t.reshape(b, seq_len, aw)
        ya = _attn_call(to_seq(q), to_seq(k), to_seq(v), tri).reshape(n_tok, aw)
        h = _outmlp_call(src, ycp, ya, w_out[i].astype(_BF16), g_mlp[i][None],
                         w_up[i].astype(_BF16), w_down[i].astype(_BF16), n_tok=n_tok,
                         seq_len=seq_len, g_final=g_final[None] if i == depth - 1 else None)
        src = (h,)
    return h
```
